```python
import jax, jax.numpy as jnp
from jax import lax
import numpy as np

D_MODEL = 2048
BATCH = 2
SEQ = 4096
DEPTH = 4

N_MIXERS = 2
N_RGLRU = (DEPTH + 1) // 2
N_RWKV = DEPTH // 2
D_FF = 4 * D_MODEL
D_RNN = D_MODEL
RG_HEADS = 8
RG_BLOCK = D_RNN // RG_HEADS
CONV_W = 4
RG_C = 8.0
RW_HEAD = 64
RW_HEADS = D_MODEL // RW_HEAD
DECAY_LORA = 96
AAA_LORA = 96
GATE_LORA = 256
RMS_EPS = 1e-6
GN_EPS = 64e-5

kernel_name = 'hybrid_rglru_rwkv7_sqrelu_trunk'


def rmsnorm(x, g):
    xf = x.astype(jnp.float32)
    y = xf * lax.rsqrt(jnp.mean(xf * xf, axis=-1, keepdims=True) + RMS_EPS)
    return (y * g.astype(jnp.float32)).astype(x.dtype)


def causal_depthwise_conv(x, w, b):
    s = x.shape[1]
    xp = jnp.pad(x, ((0, 0), (CONV_W - 1, 0), (0, 0)))
    out = b
    for k in range(CONV_W):
        out = out + w[k] * xp[:, k:k + s]
    return out


def block_diag_linear(x, w, b):
    xb = x.reshape(x.shape[:-1] + (RG_HEADS, RG_BLOCK))
    return jnp.einsum('bshi,hij->bshj', xb, w).reshape(x.shape) + b


def rglru_block(x, w_in, conv_w, conv_b, gx_w, gx_b, ga_w, ga_b, lam, w_out):
    proj = x @ w_in
    gate_branch, rec = jnp.split(proj, 2, axis=-1)
    rec = causal_depthwise_conv(rec, conv_w, conv_b)
    recf = rec.astype(jnp.float32)
    i_t = jax.nn.sigmoid(block_diag_linear(recf, gx_w, gx_b).astype(jnp.float32))
    r_t = jax.nn.sigmoid(block_diag_linear(recf, ga_w, ga_b).astype(jnp.float32))
    log_a = RG_C * r_t * jax.nn.log_sigmoid(lam.astype(jnp.float32))
    a_t = jnp.exp(log_a)
    mult = jnp.sqrt(jnp.maximum(-jnp.expm1(2.0 * log_a), 0.0))
    b_t = mult * (i_t * recf)

    def combine(e1, e2):
        a1, b1 = e1
        a2, b2 = e2
        return a1 * a2, a2 * b1 + b2

    _, h = lax.associative_scan(combine, (a_t, b_t), axis=1)
    y = h.astype(x.dtype) * jax.nn.gelu(gate_branch)
    return y @ w_out


def rwkv7_time_mix(x, mu, w_rkv, w0, w1, w2, a0, a1, a2, g1, g2,
                   k_k, k_a, r_k, ln_g, ln_b, w_out):
    bsz, s, d = x.shape
    f32 = jnp.float32
    xx = jnp.pad(x, ((0, 0), (1, 0), (0, 0)))[:, :-1] - x
    xm = x[:, :, None, :] + xx[:, :, None, :] * mu
    rkv = jnp.einsum('bsjd,jde->bsje', xm[:, :, :3], w_rkv)
    r, k, v = rkv[:, :, 0], rkv[:, :, 1], rkv[:, :, 2]
    xw, xa, xg = xm[:, :, 3], xm[:, :, 4], xm[:, :, 5]
    w = -jax.nn.softplus(-(w0 + jnp.tanh(xw @ w1) @ w2)) - 0.5
    decay = jnp.exp(-jnp.exp(w.astype(f32)))
    a = jax.nn.sigmoid((a0 + (xa @ a1) @ a2).astype(f32))
    g = jax.nn.sigmoid(xg @ g1) @ g2

    def heads(t):
        return t.astype(f32).reshape(bsz, s, RW_HEADS, RW_HEAD)

    kk = heads(k * k_k)
    kk = kk / jnp.maximum(jnp.sqrt(jnp.sum(kk * kk, axis=-1, keepdims=True)), 1e-12)
    k_mod = k.astype(f32) * (1.0 + (a - 1.0) * k_a.astype(f32))
    r_h, k_h, v_h, w_h, a_h = heads(r), heads(k_mod), heads(v), heads(decay), heads(a)

    def step(state, inp):
        r_t, w_t, k_t, v_t, kk_t, a_t = inp
        sa = jnp.einsum('bhij,bhj->bhi', state, kk_t)
        state = (state * w_t[:, :, None, :]
                 - sa[..., None] * (kk_t * a_t)[:, :, None, :]
                 + v_t[..., None] * k_t[:, :, None, :])
        y_t = jnp.einsum('bhij,bhj->bhi', state, r_t)
        return state, y_t

    xs = tuple(jnp.moveaxis(t, 1, 0) for t in (r_h, w_h, k_h, v_h, kk, a_h))
    state0 = jnp.zeros((bsz, RW_HEADS, RW_HEAD, RW_HEAD), f32)
    _, y = lax.scan(step, state0, xs)
    y = jnp.moveaxis(y, 0, 1)
    mean = jnp.mean(y, axis=-1, keepdims=True)
    var = jnp.mean(jnp.square(y - mean), axis=-1, keepdims=True)
    y = (y - mean) * lax.rsqrt(var + GN_EPS)
    y = y.reshape(bsz, s, d) * ln_g.astype(f32) + ln_b.astype(f32)
    bonus = jnp.sum(r_h * k_h * r_k.astype(f32), axis=-1, keepdims=True) * v_h
    y = y + bonus.reshape(bsz, s, d)
    return (y * g.astype(f32)).astype(x.dtype) @ w_out


def sq_relu_mlp(x, w_up, w_down):
    return jnp.square(jax.nn.relu(x @ w_up)) @ w_down


def setup_inputs(seed: int = 0) -> dict:
    key = jax.random.key(seed)
    ks = jax.random.split(key, 40)
    f32 = jnp.float32

    def nrm(k, shape, scale):
        return jax.random.normal(k, shape, f32) * scale

    x = jax.random.normal(ks[0], (BATCH, SEQ, D_MODEL), f32)
    norm_mix_g = 1.0 + nrm(ks[1], (DEPTH, D_MODEL), 0.02)
    norm_mlp_g = 1.0 + nrm(ks[2], (DEPTH, D_MODEL), 0.02)
    w_mlp_up = nrm(ks[3], (DEPTH, D_MODEL, D_FF), D_MODEL ** -0.5)
    w_mlp_down = nrm(ks[4], (DEPTH, D_FF, D_MODEL), 0.5 * D_FF ** -0.5)
    final_norm_g = 1.0 + nrm(ks[5], (D_MODEL,), 0.02)
    rg_w_in = nrm(ks[6], (N_RGLRU, D_MODEL, 2 * D_RNN), D_MODEL ** -0.5)
    rg_conv_w = nrm(ks[7], (N_RGLRU, CONV_W, D_RNN), CONV_W ** -0.5)
    rg_conv_b = nrm(ks[8], (N_RGLRU, D_RNN), 0.02)
    rg_gx_w = nrm(ks[9], (N_RGLRU, RG_HEADS, RG_BLOCK, RG_BLOCK), RG_BLOCK ** -0.5)
    rg_gx_b = nrm(ks[10], (N_RGLRU, D_RNN), 0.1)
    rg_ga_w = nrm(ks[11], (N_RGLRU, RG_HEADS, RG_BLOCK, RG_BLOCK), RG_BLOCK ** -0.5)
    rg_ga_b = nrm(ks[12], (N_RGLRU, D_RNN), 0.1)
    u = jax.random.uniform(ks[13], (N_RGLRU, D_RNN), f32, minval=0.9, maxval=0.999)
    sgm = u ** (1.0 / RG_C)
    rg_lambda = jnp.log(sgm) - jnp.log1p(-sgm)
    rg_w_out = nrm(ks[14], (N_RGLRU, D_RNN, D_MODEL), 0.5 * D_RNN ** -0.5)
    rw_mu = jax.random.uniform(ks[15], (N_RWKV, 6, D_MODEL), f32)
    rw_w_rkv = nrm(ks[16], (N_RWKV, 3, D_MODEL, D_MODEL), D_MODEL ** -0.5)
    ratio = jnp.arange(D_MODEL, dtype=f32) / (D_MODEL - 1)
    rw_w0 = -6.0 + 5.0 * ratio ** 1.35 + nrm(ks[17], (N_RWKV, D_MODEL), 0.1)
    rw_w1 = nrm(ks[18], (N_RWKV, D_MODEL, DECAY_LORA), D_MODEL ** -0.5)
    rw_w2 = nrm(ks[19], (N_RWKV, DECAY_LORA, D_MODEL), 0.3 * DECAY_LORA ** -0.5)
    rw_a0 = nrm(ks[20], (N_RWKV, D_MODEL), 0.1)
    rw_a1 = nrm(ks[21], (N_RWKV, D_MODEL, AAA_LORA), D_MODEL ** -0.5)
    rw_a2 = nrm(ks[22], (N_RWKV, AAA_LORA, D_MODEL), 0.5 * AAA_LORA ** -0.5)
    rw_g1 = nrm(ks[23], (N_RWKV, D_MODEL, GATE_LORA), D_MODEL ** -0.5)
    rw_g2 = nrm(ks[24], (N_RWKV, GATE_LORA, D_MODEL), GATE_LORA ** -0.5)
    rw_k_k = 0.85 + nrm(ks[25], (N_RWKV, D_MODEL), 0.05)
    rw_k_a = 1.0 + nrm(ks[26], (N_RWKV, D_MODEL), 0.05)
    rw_r_k = nrm(ks[27], (N_RWKV, RW_HEADS, RW_HEAD), 0.1)
    rw_ln_g = 1.0 + nrm(ks[28], (N_RWKV, D_MODEL), 0.02)
    rw_ln_b = nrm(ks[29], (N_RWKV, D_MODEL), 0.02)
    rw_w_out = nrm(ks[30], (N_RWKV, D_MODEL, D_MODEL), 0.5 * D_MODEL ** -0.5)
    return {
        'x': x, 'norm_mix_g': norm_mix_g, 'norm_mlp_g': norm_mlp_g,
        'w_mlp_up': w_mlp_up, 'w_mlp_down': w_mlp_down, 'final_norm_g': final_norm_g,
        'rg_w_in': rg_w_in, 'rg_conv_w': rg_conv_w, 'rg_conv_b': rg_conv_b,
        'rg_gx_w': rg_gx_w, 'rg_gx_b': rg_gx_b, 'rg_ga_w': rg_ga_w, 'rg_ga_b': rg_ga_b,
        'rg_lambda': rg_lambda, 'rg_w_out': rg_w_out,
        'rw_mu': rw_mu, 'rw_w_rkv': rw_w_rkv, 'rw_w0': rw_w0, 'rw_w1': rw_w1, 'rw_w2': rw_w2,
        'rw_a0': rw_a0, 'rw_a1': rw_a1, 'rw_a2': rw_a2, 'rw_g1': rw_g1, 'rw_g2': rw_g2,
        'rw_k_k': rw_k_k, 'rw_k_a': rw_k_a, 'rw_r_k': rw_r_k,
        'rw_ln_g': rw_ln_g, 'rw_ln_b': rw_ln_b, 'rw_w_out': rw_w_out,
    }


def reference(x, norm_mix_g, norm_mlp_g, w_mlp_up, w_mlp_down, final_norm_g,
              rg_w_in, rg_conv_w, rg_conv_b, rg_gx_w, rg_gx_b, rg_ga_w, rg_ga_b,
              rg_lambda, rg_w_out,
              rw_mu, rw_w_rkv, rw_w0, rw_w1, rw_w2, rw_a0, rw_a1, rw_a2, rw_g1, rw_g2,
              rw_k_k, rw_k_a, rw_r_k, rw_ln_g, rw_ln_b, rw_w_out):
    h = x
    for i in range(DEPTH):
        hn = rmsnorm(h, norm_mix_g[i])
        j = i // N_MIXERS
        if i % N_MIXERS == 0:
            mix = rglru_block(hn, rg_w_in[j], rg_conv_w[j], rg_conv_b[j],
                              rg_gx_w[j], rg_gx_b[j], rg_ga_w[j], rg_ga_b[j],
                              rg_lambda[j], rg_w_out[j])
        else:
            mix = rwkv7_time_mix(hn, rw_mu[j], rw_w_rkv[j], rw_w0[j], rw_w1[j], rw_w2[j],
                                 rw_a0[j], rw_a1[j], rw_a2[j], rw_g1[j], rw_g2[j],
                                 rw_k_k[j], rw_k_a[j], rw_r_k[j],
                                 rw_ln_g[j], rw_ln_b[j], rw_w_out[j])
        h = h + mix
        hn = rmsnorm(h, norm_mlp_g[i])
        h = h + sq_relu_mlp(hn, w_mlp_up[i], w_mlp_down[i])
    return rmsnorm(h, final_norm_g)
```

```python
import functools

import jax
import jax.numpy as jnp
from jax import lax
from jax.experimental import pallas as pl
from jax.experimental.pallas import tpu as pltpu

F32 = jnp.float32
BF16 = jnp.bfloat16
HIGHEST = lax.Precision.HIGHEST

RMS_EPS = 1e-6
GN_EPS = 64e-5
RG_C = 8.0
RG_HEADS = 8
CONV_W = 4
RW_HEAD = 64
LANES = 128
SUBLANES = 8
CHUNK = 64
VMEM_LIMIT = 56 * 1024 * 1024


def _params(sem):
    return pltpu.CompilerParams(dimension_semantics=sem, vmem_limit_bytes=VMEM_LIMIT)


def _dot(a, b):
    return jnp.dot(a.astype(BF16), b.astype(BF16), preferred_element_type=F32)


def _dot_nt(a, b):
    return lax.dot_general(a.astype(BF16), b.astype(BF16), (((1,), (1,)), ((), ())),
                           preferred_element_type=F32)


def _dot_exact(a, b):
    return jnp.dot(a, b, precision=HIGHEST, preferred_element_type=F32)


def _gelu_tanh(x):
    return 0.5 * x * (1.0 + jnp.tanh(0.7978845608028654 * (x + 0.044715 * (x * x * x))))


def _softplus(x):
    return jnp.maximum(x, 0.0) + jnp.log1p(jnp.exp(-jnp.abs(x)))


def _rmsnorm_kernel(x_ref, g_ref, o_ref):
    x = x_ref[...]
    ms = jnp.mean(x * x, axis=-1, keepdims=True)
    o_ref[...] = (x * lax.rsqrt(ms + RMS_EPS) * g_ref[...]).astype(o_ref.dtype)


def rmsnorm(x, g, out_dtype, tm=512):
    t, d = x.shape
    return pl.pallas_call(
        _rmsnorm_kernel,
        out_shape=jax.ShapeDtypeStruct((t, d), out_dtype),
        grid=(t // tm,),
        in_specs=[pl.BlockSpec((tm, d), lambda i: (i, 0)),
                  pl.BlockSpec((1, d), lambda i: (0, 0))],
        out_specs=pl.BlockSpec((tm, d), lambda i: (i, 0)),
        compiler_params=_params(("parallel",)),
        name="rmsnorm",
    )(x, g.reshape(1, d))


def _mm_act_kernel(a_ref, w_ref, o_ref, *, act):
    y = jnp.dot(a_ref[...], w_ref[...], preferred_element_type=F32)
    if act == "relu2":
        y = jnp.square(jnp.maximum(y, 0.0))
    elif act == "gelu":
        y = _gelu_tanh(y)
    o_ref[...] = y.astype(o_ref.dtype)


def mm_act(a, w, act, out_dtype, tm=1024, tn=1024):
    t, k = a.shape
    n = w.shape[1]
    return pl.pallas_call(
        functools.partial(_mm_act_kernel, act=act),
        out_shape=jax.ShapeDtypeStruct((t, n), out_dtype),
        grid=(t // tm, n // tn),
        in_specs=[pl.BlockSpec((tm, k), lambda i, j: (i, 0)),
                  pl.BlockSpec((k, tn), lambda i, j: (0, j))],
        out_specs=pl.BlockSpec((tm, tn), lambda i, j: (i, j)),
        compiler_params=_params(("parallel", "parallel")),
        name="mm_" + act,
    )(a, w)


def _mm_res_norm_kernel(a_ref, w_ref, h_ref, g_ref, *refs, emit_h):
    if emit_h:
        hnew_ref, hn_ref, acc_ref = refs
    else:
        hn_ref, acc_ref = refs
    kk = pl.program_id(1)

    @pl.when(kk == 0)
    def _():
        acc_ref[...] = jnp.zeros_like(acc_ref)

    acc_ref[...] += jnp.dot(a_ref[...], w_ref[...], preferred_element_type=F32)

    @pl.when(kk == pl.num_programs(1) - 1)
    def _():
        hnew = h_ref[...] + acc_ref[...]
        if emit_h:
            hnew_ref[...] = hnew
        ms = jnp.mean(hnew * hnew, axis=-1, keepdims=True)
        hn_ref[...] = (hnew * lax.rsqrt(ms + RMS_EPS) * g_ref[...]).astype(hn_ref.dtype)


def mm_res_norm(a, w, h, g, hn_dtype, emit_h=True, tm=512, tk=512):
    t, k = a.shape
    d = w.shape[1]
    row_spec = pl.BlockSpec((tm, d), lambda i, kk: (i, 0))
    out_shape = [jax.ShapeDtypeStruct((t, d), hn_dtype)]
    out_specs = [row_spec]
    if emit_h:
        out_shape = [jax.ShapeDtypeStruct((t, d), F32)] + out_shape
        out_specs = [row_spec] + out_specs
    res = pl.pallas_call(
        functools.partial(_mm_res_norm_kernel, emit_h=emit_h),
        out_shape=out_shape,
        grid=(t // tm, k // tk),
        in_specs=[pl.BlockSpec((tm, tk), lambda i, kk: (i, kk)),
                  pl.BlockSpec((tk, d), lambda i, kk: (kk, 0)),
                  row_spec,
                  pl.BlockSpec((1, d), lambda i, kk: (0, 0))],
        out_specs=out_specs,
        scratch_shapes=[pltpu.VMEM((tm, d), F32)],
        compiler_params=_params(("parallel", "arbitrary")),
        name="mm_res_norm",
    )(a, w, h, g.reshape(1, d))
    if emit_h:
        return res[0], res[1]
    return None, res[0]


def _rg_scan_kernel(rec_ref, gate_ref, cw_ref, cb_ref, gxw_ref, gxb_ref, gaw_ref, gab_ref,
                    lam_ref, y_ref, tail_ref, hc_ref, *, tt):
    c = pl.program_id(2)

    @pl.when(c == 0)
    def _():
        tail_ref[...] = jnp.zeros_like(tail_ref)
        hc_ref[...] = jnp.zeros_like(hc_ref)

    x = rec_ref[...]
    tail = tail_ref[...]
    row8 = lax.broadcasted_iota(jnp.int32, tail.shape, 0)
    xc = cb_ref[...] + cw_ref[CONV_W - 1:CONV_W, :] * x
    for j in range(1, CONV_W):
        xr = pltpu.roll(x, j, 0)
        tr = pltpu.roll(tail, j, 0)
        head = jnp.where(row8 < j, tr, xr[:SUBLANES])
        xs = jnp.concatenate([head, xr[SUBLANES:]], axis=0)
        xc = xc + cw_ref[CONV_W - 1 - j:CONV_W - j, :] * xs
    tail_ref[...] = x[tt - SUBLANES:]

    xb = xc.astype(BF16)
    i_t = jax.nn.sigmoid(jnp.dot(xb, gxw_ref[...], preferred_element_type=F32) + gxb_ref[...])
    r_t = jax.nn.sigmoid(jnp.dot(xb, gaw_ref[...], preferred_element_type=F32) + gab_ref[...])
    log_sig_lam = -_softplus(-lam_ref[...])
    log_a = RG_C * r_t * log_sig_lam
    a = jnp.exp(log_a)
    th = jnp.tanh(log_a)
    mult = jnp.sqrt(jnp.maximum(-2.0 * th / (1.0 - th), 0.0))
    b = mult * (i_t * xc)

    row = lax.broadcasted_iota(jnp.int32, x.shape, 0)
    s = 1
    while s < tt:
        a_sh = pltpu.roll(a, s, 0)
        b_sh = pltpu.roll(b, s, 0)
        valid = row >= s
        b = jnp.where(valid, a * b_sh + b, b)
        a = jnp.where(valid, a * a_sh, a)
        s *= 2
    h = a * hc_ref[...] + b
    hc_ref[...] = h[tt - 1:tt]
    y_ref[...] = (h * gate_ref[...]).astype(y_ref.dtype)


def rg_scan(rec, gate, conv_w, conv_b, gx_w, gx_b, ga_w, ga_b, lam, batch, tt=256):
    t, d = rec.shape
    seq = t // batch
    nc = seq // tt
    blk = d // RG_HEADS
    act_spec = pl.BlockSpec((tt, blk), lambda b, h, c: (b * nc + c, h))
    vec_spec = pl.BlockSpec((1, blk), lambda b, h, c: (0, h))
    w_spec = pl.BlockSpec((None, blk, blk), lambda b, h, c: (h, 0, 0))
    return pl.pallas_call(
        functools.partial(_rg_scan_kernel, tt=tt),
        out_shape=jax.ShapeDtypeStruct((t, d), BF16),
        grid=(batch, RG_HEADS, nc),
        in_specs=[act_spec, act_spec,
                  pl.BlockSpec((CONV_W, blk), lambda b, h, c: (0, h)), vec_spec,
                  w_spec, vec_spec, w_spec, vec_spec, vec_spec],
        out_specs=act_spec,
        scratch_shapes=[pltpu.VMEM((SUBLANES, blk), F32), pltpu.VMEM((1, blk), F32)],
        compiler_params=_params(("parallel", "parallel", "arbitrary")),
        name="rg_scan",
    )(rec, gate, conv_w, conv_b.reshape(1, d), gx_w, gx_b.reshape(1, d),
      ga_w, ga_b.reshape(1, d), lam.reshape(1, d))


def _token_shift_delta(x, prev_ref, is_first):
    prev_last = jnp.where(is_first, 0.0, prev_ref[SUBLANES - 1:SUBLANES, :])
    row = lax.broadcasted_iota(jnp.int32, x.shape, 0)
    xs = jnp.where(row == 0, prev_last, pltpu.roll(x, 1, 0))
    return xs - x


def _rw_proj_kernel(x_ref, prev_ref, mu_ref, w_ref, o_ref, *, blocks_per_seq):
    i = pl.program_id(1)
    x = x_ref[...]
    xx = _token_shift_delta(x, prev_ref, (i % blocks_per_seq) == 0)
    xm = (x + xx * mu_ref[...]).astype(BF16)
    o_ref[...] = jnp.dot(xm, w_ref[...], preferred_element_type=F32)


def rw_proj(hn, mu, w_rkv, seq, tm=512):
    t, d = hn.shape
    nmat = w_rkv.shape[0]
    return pl.pallas_call(
        functools.partial(_rw_proj_kernel, blocks_per_seq=seq // tm),
        out_shape=jax.ShapeDtypeStruct((nmat, t, d), F32),
        grid=(nmat, t // tm),
        in_specs=[pl.BlockSpec((tm, d), lambda j, i: (i, 0)),
                  pl.BlockSpec((SUBLANES, d),
                               lambda j, i: (jnp.maximum(i * (tm // SUBLANES) - 1, 0), 0)),
                  pl.BlockSpec((None, 1, d), lambda j, i: (j, 0, 0)),
                  pl.BlockSpec((None, d, d), lambda j, i: (j, 0, 0))],
        out_specs=pl.BlockSpec((None, tm, d), lambda j, i: (j, i, 0)),
        compiler_params=_params(("parallel", "parallel")),
        name="rw_proj",
    )(hn, hn, mu.reshape(nmat, 1, d), w_rkv)


def _rw_lora_kernel(x_ref, prev_ref, mu_ref, w0_ref, w1_ref, w2_ref, a0_ref, a1_ref, a2_ref,
                    g1_ref, g2_ref, lw_ref, a_ref, g_ref, *, blocks_per_seq):
    i = pl.program_id(0)
    x = x_ref[...]
    xx = _token_shift_delta(x, prev_ref, (i % blocks_per_seq) == 0)
    xw = (x + xx * mu_ref[0:1, :]).astype(BF16)
    xa = (x + xx * mu_ref[1:2, :]).astype(BF16)
    xg = (x + xx * mu_ref[2:3, :]).astype(BF16)
    hid = jnp.tanh(jnp.dot(xw, w1_ref[...], preferred_element_type=F32))
    w = w0_ref[...] + _dot(hid, w2_ref[...])
    w = -_softplus(-w) - 0.5
    lw_ref[...] = -jnp.exp(w)
    hid = jnp.dot(xa, a1_ref[...], preferred_element_type=F32)
    a_ref[...] = jax.nn.sigmoid(a0_ref[...] + _dot(hid, a2_ref[...]))
    hid = jax.nn.sigmoid(jnp.dot(xg, g1_ref[...], preferred_element_type=F32))
    g_ref[...] = _dot(hid, g2_ref[...])


def rw_lora(hn, mu3, w0, w1, w2, a0, a1, a2, g1, g2, seq, tm=512):
    t, d = hn.shape
    row_spec = pl.BlockSpec((tm, d), lambda i: (i, 0))

    def full(arr):
        return pl.BlockSpec(arr.shape, lambda i: (0,) * arr.ndim)

    w0, a0 = w0.reshape(1, d), a0.reshape(1, d)
    consts = (mu3, w0, w1, w2, a0, a1, a2, g1, g2)
    return pl.pallas_call(
        functools.partial(_rw_lora_kernel, blocks_per_seq=seq // tm),
        out_shape=[jax.ShapeDtypeStruct((t, d), F32)] * 3,
        grid=(t // tm,),
        in_specs=[row_spec,
                  pl.BlockSpec((SUBLANES, d),
                               lambda i: (jnp.maximum(i * (tm // SUBLANES) - 1, 0), 0))]
                 + [full(c) for c in consts],
        out_specs=[row_spec] * 3,
        compiler_params=_params(("parallel",)),
        name="rw_lora",
    )(hn, hn, *consts)


def _unit_lower_inverse(lp, row, col, eye):
    r8, c8 = jnp.right_shift(row, 3), jnp.right_shift(col, 3)
    ld = jnp.where(r8 == c8, lp, 0.0)
    l2 = _dot(ld, ld)
    l4 = _dot(l2, l2)
    t = _dot(eye - ld, eye + l2)
    t = _dot(t, eye + l4)
    for sh in (3, 4, 5):
        rb, cb = jnp.right_shift(row, sh), jnp.right_shift(col, sh)
        off = jnp.where((jnp.right_shift(rb, 1) == jnp.right_shift(cb, 1)) & (rb != cb), lp, 0.0)
        t = t - _dot(t, _dot(off, t))
    return t


def _rw_rec_kernel(r_ref, k_ref, v_ref, lw_ref, a_ref, g_ref, kk_ref, ka_ref, rk_ref,
                   lng_ref, lnb_ref, o_ref, s_ref, *, n_chunks, n_pairs):
    c = pl.program_id(2)

    @pl.when(c == 0)
    def _():
        s_ref[...] = jnp.zeros_like(s_ref)

    n2 = 2 * CHUNK
    row = lax.broadcasted_iota(jnp.int32, (n2, n2), 0)
    col = lax.broadcasted_iota(jnp.int32, (n2, n2), 1)
    strict = row > col
    incl = row >= col
    eye = (row == col).astype(F32)
    head_ones = (jnp.right_shift(row, 6) == jnp.right_shift(col, 6)).astype(F32)
    tri = (lax.broadcasted_iota(jnp.int32, (CHUNK, CHUNK), 0)
           >= lax.broadcasted_iota(jnp.int32, (CHUNK, CHUNK), 1)).astype(F32)
    lane = lax.broadcasted_iota(jnp.int32, (CHUNK, n2), 1)
    m0 = (lane < RW_HEAD).astype(F32)
    m1 = 1.0 - m0

    def stack(x):
        return jnp.concatenate([x * m0, x * m1], axis=0)

    for p in range(n_pairs):
        ls = slice(LANES * p, LANES * (p + 1))
        k_k, k_a, r_k = kk_ref[:, ls], ka_ref[:, ls], rk_ref[:, ls]
        ln_g, ln_b = lng_ref[:, ls], lnb_ref[:, ls]
        state = s_ref[p]
        for ci in range(n_chunks):
            rs = slice(CHUNK * ci, CHUNK * (ci + 1))
            r, k, v = r_ref[rs, ls], k_ref[rs, ls], v_ref[rs, ls]
            lw, a, g = lw_ref[rs, ls], a_ref[rs, ls], g_ref[rs, ls]

            kkp = k * k_k
            ssq = _dot_exact(kkp * kkp, head_ones)
            kap = kkp / jnp.maximum(jnp.sqrt(ssq), 1e-12)
            b = kap * a
            km = k * (1.0 + (a - 1.0) * k_a)

            cum = _dot_exact(tri, lw)
            e_in = jnp.exp(cum)
            e_neg = jnp.exp(-cum)
            kts = stack(kap * jnp.exp(cum - lw))
            bts = stack(b * e_neg)
            kms = stack(km * e_neg)
            rts = stack(r * e_in)
            vs = stack(v)
            pc = e_in[CHUNK - 1:CHUNK, :]

            amat = _dot_nt(jnp.concatenate([kts, rts], axis=0),
                           jnp.concatenate([bts, kms], axis=0))
            lp = jnp.where(strict, amat[:n2, :n2], 0.0)
            mp = jnp.where(strict, amat[:n2, n2:], 0.0)
            arb = jnp.where(incl, amat[n2:, :n2], 0.0)
            ark = jnp.where(incl, amat[n2:, n2:], 0.0)
            tp = _unit_lower_inverse(lp, row, col, eye)

            zs = _dot(tp, _dot_nt(kts, state) + _dot(mp, vs))
            ys = _dot_nt(rts, state) + _dot(ark, vs) - _dot(arb, zs)
            y = ys[:CHUNK] + ys[CHUNK:]
            state = (state + _dot(vs.T, kms) - _dot(zs.T, bts)) * pc

            mean = _dot_exact(y, head_ones) * (1.0 / RW_HEAD)
            dlt = y - mean
            var = _dot_exact(dlt * dlt, head_ones) * (1.0 / RW_HEAD)
            yn = dlt * lax.rsqrt(var + GN_EPS) * ln_g + ln_b
            bonus = _dot_exact(r * km * r_k, head_ones) * v
            o_ref[rs, ls] = ((yn + bonus) * g).astype(o_ref.dtype)
        s_ref[p] = state


def rw_rec(rkv, lw, a, g, k_k, k_a, r_k, ln_g, ln_b, batch, n_chunks=2, n_pairs=2):
    _, t, d = rkv.shape
    seq = t // batch
    tt = n_chunks * CHUNK
    lwid = n_pairs * LANES
    nc = seq // tt

    def rkv_spec(m):
        return pl.BlockSpec((None, tt, lwid), lambda b, l, c: (m, b * nc + c, l))

    act_spec = pl.BlockSpec((tt, lwid), lambda b, l, c: (b * nc + c, l))
    vec_spec = pl.BlockSpec((1, lwid), lambda b, l, c: (0, l))
    vecs = [x.reshape(1, d) for x in (k_k, k_a, r_k, ln_g, ln_b)]
    return pl.pallas_call(
        functools.partial(_rw_rec_kernel, n_chunks=n_chunks, n_pairs=n_pairs),
        out_shape=jax.ShapeDtypeStruct((t, d), BF16),
        grid=(batch, d // lwid, nc),
        in_specs=[rkv_spec(0), rkv_spec(1), rkv_spec(2), act_spec, act_spec, act_spec]
                 + [vec_spec] * 5,
        out_specs=act_spec,
        scratch_shapes=[pltpu.VMEM((n_pairs, LANES, LANES), F32)],
        compiler_params=_params(("parallel", "parallel", "arbitrary")),
        name="rw_rec",
    )(rkv, rkv, rkv, lw, a, g, *vecs)


def _pad_axis(x, axis, size):
    pad = [(0, 0)] * x.ndim
    pad[axis] = (0, size - x.shape[axis])
    return jnp.pad(x, pad)


def kernel(x, norm_mix_g, norm_mlp_g, w_mlp_up, w_mlp_down, final_norm_g, rg_w_in, rg_conv_w, rg_conv_b, rg_gx_w, rg_gx_b, rg_ga_w, rg_ga_b, rg_lambda, rg_w_out, rw_mu, rw_w_rkv, rw_w0, rw_w1, rw_w2, rw_a0, rw_a1, rw_a2, rw_g1, rw_g2, rw_k_k, rw_k_a, rw_r_k, rw_ln_g, rw_ln_b, rw_w_out):
    batch, seq, d = x.shape
    depth = norm_mix_g.shape[0]
    t = batch * seq
    h = x.reshape(t, d)
    hn = rmsnorm(h, norm_mix_g[0], BF16)
    for i in range(depth):
        j = i // 2
        if i % 2 == 0:
            w_in = rg_w_in[j].astype(BF16)
            gate = mm_act(hn, w_in[:, :d], "gelu", F32)
            rec = mm_act(hn, w_in[:, d:], "none", F32)
            y = rg_scan(rec, gate, rg_conv_w[j], rg_conv_b[j],
                        rg_gx_w[j].astype(BF16), rg_gx_b[j],
                        rg_ga_w[j].astype(BF16), rg_ga_b[j], rg_lambda[j], batch)
            w_out = rg_w_out[j].astype(BF16)
        else:
            rkv = rw_proj(hn, rw_mu[j, :3], rw_w_rkv[j].astype(BF16), seq)
            lora = LANES
            lw, a, g = rw_lora(
                hn, rw_mu[j, 3:], rw_w0[j],
                _pad_axis(rw_w1[j], 1, lora).astype(BF16), _pad_axis(rw_w2[j], 0, lora).astype(BF16),
                rw_a0[j],
                _pad_axis(rw_a1[j], 1, lora).astype(BF16), _pad_axis(rw_a2[j], 0, lora).astype(BF16),
                rw_g1[j].astype(BF16), rw_g2[j].astype(BF16), seq)
            y = rw_rec(rkv, lw, a, g, rw_k_k[j], rw_k_a[j], rw_r_k[j].reshape(d),
                       rw_ln_g[j], rw_ln_b[j], batch)
            w_out = rw_w_out[j].astype(BF16)
        h, hn = mm_res_norm(y, w_out, h, norm_mlp_g[i], BF16)
        u = mm_act(hn, w_mlp_up[i].astype(BF16), "relu2", BF16)
        last = i + 1 == depth
        g_next = final_norm_g if last else norm_mix_g[i + 1]
        hn_dtype = F32 if (last or (i + 1) % 2 == 1) else BF16
        h, hn = mm_res_norm(u, w_mlp_down[i].astype(BF16), h, g_next, hn_dtype, emit_h=not last)
    return hn.reshape(batch, seq, d)
```

```python
import functools

import jax
import jax.numpy as jnp
from jax import lax
from jax.experimental import pallas as pl
from jax.experimental.pallas import tpu as pltpu

F32 = jnp.float32
BF16 = jnp.bfloat16

RMS_EPS = 1e-6
GN_EPS = 64e-5
RG_C = 8.0
RG_HEADS = 8
CONV_W = 4
RW_HEAD = 64
LANES = 128
SUBLANES = 8
CHUNK = 64
VMEM_LIMIT = 56 * 1024 * 1024


def _params(sem):
    return pltpu.CompilerParams(dimension_semantics=sem, vmem_limit_bytes=VMEM_LIMIT)


def _dot(a, b):
    return jnp.dot(a.astype(BF16), b.astype(BF16), preferred_element_type=F32)


def _dot_nt(a, b):
    return lax.dot_general(a.astype(BF16), b.astype(BF16), (((1,), (1,)), ((), ())),
                           preferred_element_type=F32)


def _gelu_tanh(x):
    return 0.5 * x * (1.0 + jnp.tanh(0.7978845608028654 * (x + 0.044715 * (x * x * x))))


def _softplus(x):
    return jnp.maximum(x, 0.0) + jnp.log1p(jnp.exp(-jnp.abs(x)))


def _rmsnorm_kernel(x_ref, g_ref, o_ref):
    x = x_ref[...]
    ms = jnp.mean(x * x, axis=-1, keepdims=True)
    o_ref[...] = (x * lax.rsqrt(ms + RMS_EPS) * g_ref[...]).astype(o_ref.dtype)


def rmsnorm(x, g, out_dtype, tm=512):
    t, d = x.shape
    return pl.pallas_call(
        _rmsnorm_kernel,
        out_shape=jax.ShapeDtypeStruct((t, d), out_dtype),
        grid=(t // tm,),
        in_specs=[pl.BlockSpec((tm, d), lambda i: (i, 0)),
                  pl.BlockSpec((1, d), lambda i: (0, 0))],
        out_specs=pl.BlockSpec((tm, d), lambda i: (i, 0)),
        compiler_params=_params(("parallel",)),
        name="rmsnorm",
    )(x, g.reshape(1, d))


def _mm_act_kernel(a_ref, w_ref, o_ref, *, act):
    y = jnp.dot(a_ref[...], w_ref[...], preferred_element_type=F32)
    if act == "relu2":
        y = jnp.square(jnp.maximum(y, 0.0))
    elif act == "gelu":
        y = _gelu_tanh(y)
    o_ref[...] = y.astype(o_ref.dtype)


def mm_act(a, w, act, out_dtype, tm=1024, tn=1024):
    t, k = a.shape
    n = w.shape[1]
    return pl.pallas_call(
        functools.partial(_mm_act_kernel, act=act),
        out_shape=jax.ShapeDtypeStruct((t, n), out_dtype),
        grid=(t // tm, n // tn),
        in_specs=[pl.BlockSpec((tm, k), lambda i, j: (i, 0)),
                  pl.BlockSpec((k, tn), lambda i, j: (0, j))],
        out_specs=pl.BlockSpec((tm, tn), lambda i, j: (i, j)),
        compiler_params=_params(("parallel", "parallel")),
        name="mm_" + act,
    )(a, w)


def _mm_res_norm_kernel(a_ref, w_ref, h_ref, g_ref, *refs, emit_h, nk):
    refs = list(refs)
    hnew_ref = refs.pop(0) if emit_h else None
    hn_ref = refs.pop(0)
    kk = pl.program_id(1)
    part = jnp.dot(a_ref[...], w_ref[...], preferred_element_type=F32)

    def finish(acc):
        hnew = h_ref[...] + acc
        if emit_h:
            hnew_ref[...] = hnew
        ms = jnp.mean(hnew * hnew, axis=-1, keepdims=True)
        hn_ref[...] = (hnew * lax.rsqrt(ms + RMS_EPS) * g_ref[...]).astype(hn_ref.dtype)

    if nk == 1:
        finish(part)
        return
    acc_ref = refs.pop(0)

    @pl.when(kk == 0)
    def _():
        acc_ref[...] = part

    @pl.when((kk > 0) & (kk < nk - 1))
    def _():
        acc_ref[...] += part

    @pl.when(kk == nk - 1)
    def _():
        finish(acc_ref[...] + part)


def mm_res_norm(a, w, h, g, hn_dtype, emit_h=True, tm=512, tk=1024):
    t, k = a.shape
    d = w.shape[1]
    nk = k // tk
    row_spec = pl.BlockSpec((tm, d), lambda i, kk: (i, 0))
    out_shape = [jax.ShapeDtypeStruct((t, d), hn_dtype)]
    out_specs = [row_spec]
    if emit_h:
        out_shape = [jax.ShapeDtypeStruct((t, d), F32)] + out_shape
        out_specs = [row_spec] + out_specs
    res = pl.pallas_call(
        functools.partial(_mm_res_norm_kernel, emit_h=emit_h, nk=nk),
        out_shape=out_shape,
        grid=(t // tm, nk),
        in_specs=[pl.BlockSpec((tm, tk), lambda i, kk: (i, kk)),
                  pl.BlockSpec((tk, d), lambda i, kk: (kk, 0)),
                  row_spec,
                  pl.BlockSpec((1, d), lambda i, kk: (0, 0))],
        out_specs=out_specs,
        scratch_shapes=[pltpu.VMEM((tm, d), F32)] if nk > 1 else [],
        compiler_params=_params(("parallel", "arbitrary")),
        name="mm_res_norm",
    )(a, w, h, g.reshape(1, d))
    if emit_h:
        return res[0], res[1]
    return None, res[0]


def _rg_scan_kernel(rec_ref, gate_ref, cw_ref, cb_ref, gxw_ref, gxb_ref, gaw_ref, gab_ref,
                    lam_ref, y_ref, tail_ref, hc_ref, *, tt):
    c = pl.program_id(2)

    @pl.when(c == 0)
    def _():
        tail_ref[...] = jnp.zeros_like(tail_ref)
        hc_ref[...] = jnp.zeros_like(hc_ref)

    x = rec_ref[...]
    tail = tail_ref[...]
    row8 = lax.broadcasted_iota(jnp.int32, tail.shape, 0)
    xc = cb_ref[...] + cw_ref[CONV_W - 1:CONV_W, :] * x
    for j in range(1, CONV_W):
        xr = pltpu.roll(x, j, 0)
        tr = pltpu.roll(tail, j, 0)
        head = jnp.where(row8 < j, tr, xr[:SUBLANES])
        xs = jnp.concatenate([head, xr[SUBLANES:]], axis=0)
        xc = xc + cw_ref[CONV_W - 1 - j:CONV_W - j, :] * xs
    tail_ref[...] = x[tt - SUBLANES:]

    xb = xc.astype(BF16)
    i_t = jax.nn.sigmoid(jnp.dot(xb, gxw_ref[...], preferred_element_type=F32) + gxb_ref[...])
    r_t = jax.nn.sigmoid(jnp.dot(xb, gaw_ref[...], preferred_element_type=F32) + gab_ref[...])
    log_sig_lam = -_softplus(-lam_ref[...])
    log_a = RG_C * r_t * log_sig_lam
    a = jnp.exp(log_a)
    th = jnp.tanh(log_a)
    mult = jnp.sqrt(jnp.maximum(-2.0 * th / (1.0 - th), 0.0))
    b = mult * (i_t * xc)

    row = lax.broadcasted_iota(jnp.int32, x.shape, 0)
    s = 1
    while s < tt:
        a_sh = pltpu.roll(a, s, 0)
        b_sh = pltpu.roll(b, s, 0)
        valid = row >= s
        b = jnp.where(valid, a * b_sh + b, b)
        a = jnp.where(valid, a * a_sh, a)
        s *= 2
    h = a * hc_ref[...] + b
    hc_ref[...] = h[tt - 1:tt]
    y_ref[...] = (h * gate_ref[...]).astype(y_ref.dtype)


def rg_scan(rec, gate, conv_w, conv_b, gx_w, gx_b, ga_w, ga_b, lam, batch, tt=256):
    t, d = rec.shape
    seq = t // batch
    nc = seq // tt
    blk = d // RG_HEADS
    act_spec = pl.BlockSpec((tt, blk), lambda b, h, c: (b * nc + c, h))
    vec_spec = pl.BlockSpec((1, blk), lambda b, h, c: (0, h))
    w_spec = pl.BlockSpec((None, blk, blk), lambda b, h, c: (h, 0, 0))
    return pl.pallas_call(
        functools.partial(_rg_scan_kernel, tt=tt),
        out_shape=jax.ShapeDtypeStruct((t, d), BF16),
        grid=(batch, RG_HEADS, nc),
        in_specs=[act_spec, act_spec,
                  pl.BlockSpec((CONV_W, blk), lambda b, h, c: (0, h)), vec_spec,
                  w_spec, vec_spec, w_spec, vec_spec, vec_spec],
        out_specs=act_spec,
        scratch_shapes=[pltpu.VMEM((SUBLANES, blk), F32), pltpu.VMEM((1, blk), F32)],
        compiler_params=_params(("parallel", "parallel", "arbitrary")),
        name="rg_scan",
    )(rec, gate, conv_w, conv_b.reshape(1, d), gx_w, gx_b.reshape(1, d),
      ga_w, ga_b.reshape(1, d), lam.reshape(1, d))


def _token_shift_delta(x, prev_ref, is_first):
    prev_last = jnp.where(is_first, 0.0, prev_ref[SUBLANES - 1:SUBLANES, :])
    row = lax.broadcasted_iota(jnp.int32, x.shape, 0)
    xs = jnp.where(row == 0, prev_last, pltpu.roll(x, 1, 0))
    return xs - x


def _rw_proj_kernel(x_ref, prev_ref, mu_ref, w_ref, o_ref, *, blocks_per_seq):
    i = pl.program_id(1)
    x = x_ref[...]
    xx = _token_shift_delta(x, prev_ref, (i % blocks_per_seq) == 0)
    xm = (x + xx * mu_ref[...]).astype(BF16)
    o_ref[...] = jnp.dot(xm, w_ref[...], preferred_element_type=F32)


def rw_proj(hn, mu, w_rkv, seq, tm=512):
    t, d = hn.shape
    nmat = w_rkv.shape[0]
    return pl.pallas_call(
        functools.partial(_rw_proj_kernel, blocks_per_seq=seq // tm),
        out_shape=jax.ShapeDtypeStruct((nmat, t, d), F32),
        grid=(nmat, t // tm),
        in_specs=[pl.BlockSpec((tm, d), lambda j, i: (i, 0)),
                  pl.BlockSpec((SUBLANES, d),
                               lambda j, i: (jnp.maximum(i * (tm // SUBLANES) - 1, 0), 0)),
                  pl.BlockSpec((None, 1, d), lambda j, i: (j, 0, 0)),
                  pl.BlockSpec((None, d, d), lambda j, i: (j, 0, 0))],
        out_specs=pl.BlockSpec((None, tm, d), lambda j, i: (j, i, 0)),
        compiler_params=_params(("parallel", "parallel")),
        name="rw_proj",
    )(hn, hn, mu.reshape(nmat, 1, d), w_rkv)


def _rw_lora_kernel(x_ref, prev_ref, mu_ref, w0_ref, w1_ref, w2_ref, a0_ref, a1_ref, a2_ref,
                    g1_ref, g2_ref, lw_ref, a_ref, g_ref, *, blocks_per_seq):
    i = pl.program_id(0)
    x = x_ref[...]
    xx = _token_shift_delta(x, prev_ref, (i % blocks_per_seq) == 0)
    xw = (x + xx * mu_ref[0:1, :]).astype(BF16)
    xa = (x + xx * mu_ref[1:2, :]).astype(BF16)
    xg = (x + xx * mu_ref[2:3, :]).astype(BF16)
    hid = jnp.tanh(jnp.dot(xw, w1_ref[...], preferred_element_type=F32))
    w = w0_ref[...] + _dot(hid, w2_ref[...])
    w = -_softplus(-w) - 0.5
    lw_ref[...] = -jnp.exp(w)
    hid = jnp.dot(xa, a1_ref[...], preferred_element_type=F32)
    a_ref[...] = jax.nn.sigmoid(a0_ref[...] + _dot(hid, a2_ref[...]))
    hid = jax.nn.sigmoid(jnp.dot(xg, g1_ref[...], preferred_element_type=F32))
    g_ref[...] = _dot(hid, g2_ref[...])


def rw_lora(hn, mu3, w0, w1, w2, a0, a1, a2, g1, g2, seq, tm=512):
    t, d = hn.shape
    row_spec = pl.BlockSpec((tm, d), lambda i: (i, 0))

    def full(arr):
        return pl.BlockSpec(arr.shape, lambda i: (0,) * arr.ndim)

    w0, a0 = w0.reshape(1, d), a0.reshape(1, d)
    consts = (mu3, w0, w1, w2, a0, a1, a2, g1, g2)
    return pl.pallas_call(
        functools.partial(_rw_lora_kernel, blocks_per_seq=seq // tm),
        out_shape=[jax.ShapeDtypeStruct((t, d), F32)] * 3,
        grid=(t // tm,),
        in_specs=[row_spec,
                  pl.BlockSpec((SUBLANES, d),
                               lambda i: (jnp.maximum(i * (tm // SUBLANES) - 1, 0), 0))]
                 + [full(c) for c in consts],
        out_specs=[row_spec] * 3,
        compiler_params=_params(("parallel",)),
        name="rw_lora",
    )(hn, hn, *consts)


def _split_terms(x, passes):
    terms = []
    for i in range(passes):
        part = x.astype(BF16)
        terms.append(part)
        if i + 1 < passes:
            x = x - part.astype(F32)
    return terms


def _unit_lower_inverse_many(lps, row, col, eye):
    diag8 = jnp.right_shift(row, 3) == jnp.right_shift(col, 3)
    lds = [jnp.where(diag8, lp, 0.0) for lp in lps]
    l2s = [_dot(ld, ld) for ld in lds]
    l4s = [_dot(l2, l2) for l2 in l2s]
    ts = [_dot(eye - ld, eye + l2) for ld, l2 in zip(lds, l2s)]
    ts = [_dot(t, eye + l4) for t, l4 in zip(ts, l4s)]
    for sh in (3, 4, 5):
        rb, cb = jnp.right_shift(row, sh), jnp.right_shift(col, sh)
        sel = (jnp.right_shift(rb, 1) == jnp.right_shift(cb, 1)) & (rb != cb)
        ots = [_dot(jnp.where(sel, lp, 0.0), t) for lp, t in zip(lps, ts)]
        ts = [t - _dot(t, ot) for t, ot in zip(ts, ots)]
    return ts


def _rw_rec_kernel(r_ref, k_ref, v_ref, lw_ref, a_ref, g_ref, kk_ref, ka_ref, rk_ref,
                   lng_ref, lnb_ref, o_ref, s_ref, y_ref, *, n_chunks, n_pairs):
    c = pl.program_id(2)

    @pl.when(c == 0)
    def _():
        s_ref[...] = jnp.zeros_like(s_ref)

    tt, lwid = n_chunks * CHUNK, n_pairs * LANES
    n2 = 2 * CHUNK
    shift = CHUNK.bit_length() - 1
    row = lax.broadcasted_iota(jnp.int32, (n2, n2), 0)
    col = lax.broadcasted_iota(jnp.int32, (n2, n2), 1)
    strict = row > col
    incl = row >= col
    eye = (row == col).astype(F32)
    trow = lax.broadcasted_iota(jnp.int32, (tt, tt), 0)
    tcol = lax.broadcasted_iota(jnp.int32, (tt, tt), 1)
    tri = ((jnp.right_shift(trow, shift) == jnp.right_shift(tcol, shift))
           & (trow >= tcol)).astype(BF16)
    hrow = lax.broadcasted_iota(jnp.int32, (lwid, lwid), 0)
    hcol = lax.broadcasted_iota(jnp.int32, (lwid, lwid), 1)
    head_ones = (jnp.right_shift(hrow, shift) == jnp.right_shift(hcol, shift)).astype(BF16)
    lane = lax.broadcasted_iota(jnp.int32, (CHUNK, n2), 1)
    m0 = (lane < RW_HEAD).astype(F32)
    m1 = 1.0 - m0

    def head_sum(x, passes=2):
        return sum(jnp.dot(p, head_ones, preferred_element_type=F32)
                   for p in _split_terms(x, passes))

    def stack(x):
        return jnp.concatenate([x * m0, x * m1], axis=0)

    r, k, v = r_ref[...], k_ref[...], v_ref[...]
    lw, a = lw_ref[...], a_ref[...]
    kkp = k * kk_ref[...]
    kap = kkp / jnp.maximum(jnp.sqrt(head_sum(kkp * kkp)), 1e-12)
    b = kap * a
    km = k * (1.0 + (a - 1.0) * ka_ref[...])
    cum = sum(jnp.dot(tri, p, preferred_element_type=F32) for p in _split_terms(lw, 3))
    e_in = jnp.exp(cum)
    e_neg = jnp.exp(-cum)
    kt = kap * jnp.exp(cum - lw)
    bt = b * e_neg
    kmt = km * e_neg
    rt = r * e_in

    items = [(ci, p) for ci in range(n_chunks) for p in range(n_pairs)]

    def blk(x, ci, p):
        return x[CHUNK * ci:CHUNK * (ci + 1), LANES * p:LANES * (p + 1)]

    kts = [stack(blk(kt, *it)).astype(BF16) for it in items]
    bts = [stack(blk(bt, *it)).astype(BF16) for it in items]
    kms = [stack(blk(kmt, *it)).astype(BF16) for it in items]
    rts = [stack(blk(rt, *it)) for it in items]
    vs = [stack(blk(v, *it)) for it in items]
    vsb = [x.astype(BF16) for x in vs]
    pcs = [blk(e_in, *it)[CHUNK - 1:CHUNK, :] for it in items]
    kbs = [jnp.concatenate([km_, bt_], axis=0) for km_, bt_ in zip(kms, bts)]
    amats = [_dot_nt(jnp.concatenate([kt_, rt_.astype(BF16)], axis=0), kb)
             for kt_, rt_, kb in zip(kts, rts, kbs)]
    lps = [jnp.where(strict, am[:n2, n2:], 0.0) for am in amats]
    mps = [jnp.where(strict, am[:n2, :n2], 0.0).astype(BF16) for am in amats]
    arbs = [jnp.where(incl, am[n2:, n2:], 0.0).astype(BF16) for am in amats]
    arks = [jnp.where(incl, am[n2:, :n2], 0.0).astype(BF16) for am in amats]
    tps = [t.astype(BF16) for t in _unit_lower_inverse_many(lps, row, col, eye)]
    mvs = [jnp.dot(mp, v_, preferred_element_type=F32) for mp, v_ in zip(mps, vsb)]
    wus = [jnp.dot(tp, jnp.concatenate([kt_, mv.astype(BF16)], axis=1), preferred_element_type=F32)
           for tp, kt_, mv in zip(tps, kts, mvs)]
    awus = [jnp.dot(arb, wu.astype(BF16), preferred_element_type=F32) for arb, wu in zip(arbs, wus)]
    qs = [(rt_ - awu[:, :n2]).astype(BF16) for rt_, awu in zip(rts, awus)]
    ylocs = [jnp.dot(ark, v_, preferred_element_type=F32) - awu[:, n2:]
             for ark, v_, awu in zip(arks, vsb, awus)]
    gms = [(-_dot(wu[:, :n2].T, bt_) * pc).astype(BF16) for wu, bt_, pc in zip(wus, bts, pcs)]
    hms = [_dot(jnp.concatenate([v_, -wu[:, n2:]], axis=0).T, kb) * pc
           for v_, wu, kb, pc in zip(vs, wus, kbs, pcs)]

    states = [s_ref[p] for p in range(n_pairs)]
    for idx, (ci, p) in enumerate(items):
        sb = states[p].astype(BF16)
        ys = _dot_nt(qs[idx], sb) + ylocs[idx]
        y_ref[CHUNK * ci:CHUNK * (ci + 1), LANES * p:LANES * (p + 1)] = ys[:CHUNK] + ys[CHUNK:]
        states[p] = (states[p] * pcs[idx] + jnp.dot(sb, gms[idx], preferred_element_type=F32)
                     + hms[idx])
    for p in range(n_pairs):
        s_ref[p] = states[p]

    y = y_ref[...]
    dlt = y - head_sum(y) * (1.0 / RW_HEAD)
    var = head_sum(dlt * dlt) * (1.0 / RW_HEAD)
    yn = dlt * lax.rsqrt(var + GN_EPS) * lng_ref[...] + lnb_ref[...]
    bonus = head_sum(r * km * rk_ref[...]) * v
    o_ref[...] = ((yn + bonus) * g_ref[...]).astype(o_ref.dtype)


def rw_rec(rkv, lw, a, g, k_k, k_a, r_k, ln_g, ln_b, batch, n_chunks=4, n_pairs=2):
    _, t, d = rkv.shape
    seq = t // batch
    tt = n_chunks * CHUNK
    lwid = n_pairs * LANES
    nc = seq // tt

    def rkv_spec(m):
        return pl.BlockSpec((None, tt, lwid), lambda b, l, c: (m, b * nc + c, l))

    act_spec = pl.BlockSpec((tt, lwid), lambda b, l, c: (b * nc + c, l))
    vec_spec = pl.BlockSpec((1, lwid), lambda b, l, c: (0, l))
    vecs = [x.reshape(1, d) for x in (k_k, k_a, r_k, ln_g, ln_b)]
    return pl.pallas_call(
        functools.partial(_rw_rec_kernel, n_chunks=n_chunks, n_pairs=n_pairs),
        out_shape=jax.ShapeDtypeStruct((t, d), BF16),
        grid=(batch, d // lwid, nc),
        in_specs=[rkv_spec(0), rkv_spec(1), rkv_spec(2), act_spec, act_spec, act_spec]
                 + [vec_spec] * 5,
        out_specs=act_spec,
        scratch_shapes=[pltpu.VMEM((n_pairs, LANES, LANES), F32), pltpu.VMEM((tt, lwid), F32)],
        compiler_params=_params(("parallel", "parallel", "arbitrary")),
        name="rw_rec",
    )(rkv, rkv, rkv, lw, a, g, *vecs)


def _pad_axis(x, axis, size):
    pad = [(0, 0)] * x.ndim
    pad[axis] = (0, size - x.shape[axis])
    return jnp.pad(x, pad)


def kernel(x, norm_mix_g, norm_mlp_g, w_mlp_up, w_mlp_down, final_norm_g, rg_w_in, rg_conv_w, rg_conv_b, rg_gx_w, rg_gx_b, rg_ga_w, rg_ga_b, rg_lambda, rg_w_out, rw_mu, rw_w_rkv, rw_w0, rw_w1, rw_w2, rw_a0, rw_a1, rw_a2, rw_g1, rw_g2, rw_k_k, rw_k_a, rw_r_k, rw_ln_g, rw_ln_b, rw_w_out):
    batch, seq, d = x.shape
    depth = norm_mix_g.shape[0]
    t = batch * seq
    h = x.reshape(t, d)
    hn = rmsnorm(h, norm_mix_g[0], BF16)
    for i in range(depth):
        j = i // 2
        if i % 2 == 0:
            w_in = rg_w_in[j].astype(BF16)
            gate = mm_act(hn, w_in[:, :d], "gelu", F32)
            rec = mm_act(hn, w_in[:, d:], "none", F32)
            y = rg_scan(rec, gate, rg_conv_w[j], rg_conv_b[j],
                        rg_gx_w[j].astype(BF16), rg_gx_b[j],
                        rg_ga_w[j].astype(BF16), rg_ga_b[j], rg_lambda[j], batch)
            w_out = rg_w_out[j].astype(BF16)
        else:
            rkv = rw_proj(hn, rw_mu[j, :3], rw_w_rkv[j].astype(BF16), seq)
            lora = LANES
            lw, a, g = rw_lora(
                hn, rw_mu[j, 3:], rw_w0[j],
                _pad_axis(rw_w1[j], 1, lora).astype(BF16), _pad_axis(rw_w2[j], 0, lora).astype(BF16),
                rw_a0[j],
                _pad_axis(rw_a1[j], 1, lora).astype(BF16), _pad_axis(rw_a2[j], 0, lora).astype(BF16),
                rw_g1[j].astype(BF16), rw_g2[j].astype(BF16), seq)
            y = rw_rec(rkv, lw, a, g, rw_k_k[j], rw_k_a[j], rw_r_k[j].reshape(d),
                       rw_ln_g[j], rw_ln_b[j], batch)
            w_out = rw_w_out[j].astype(BF16)
        h, hn = mm_res_norm(y, w_out, h, norm_mlp_g[i], BF16, tk=d)
        u = mm_act(hn, w_mlp_up[i].astype(BF16), "relu2", BF16)
        last = i + 1 == depth
        g_next = final_norm_g if last else norm_mix_g[i + 1]
        hn_dtype = F32 if (last or (i + 1) % 2 == 1) else BF16
        h, hn = mm_res_norm(u, w_mlp_down[i].astype(BF16), h, g_next, hn_dtype, emit_h=not last)
    return hn.reshape(batch, seq, d)
```

```python
import functools

import jax
import jax.numpy as jnp
from jax import lax
from jax.experimental import pallas as pl
from jax.experimental.pallas import tpu as pltpu

F32 = jnp.float32
BF16 = jnp.bfloat16

RMS_EPS = 1e-6
GN_EPS = 64e-5
RG_C = 8.0
RG_HEADS = 8
CONV_W = 4
RW_HEAD = 64
LANES = 128
SUBLANES = 8
CHUNK = 64
VMEM_LIMIT = 56 * 1024 * 1024


def _params(sem):
    return pltpu.CompilerParams(dimension_semantics=sem, vmem_limit_bytes=VMEM_LIMIT)


def _dot(a, b):
    return jnp.dot(a.astype(BF16), b.astype(BF16), preferred_element_type=F32)


def _dot_nt(a, b):
    return lax.dot_general(a.astype(BF16), b.astype(BF16), (((1,), (1,)), ((), ())),
                           preferred_element_type=F32)


def _gelu_tanh(x):
    return 0.5 * x * (1.0 + jnp.tanh(0.7978845608028654 * (x + 0.044715 * (x * x * x))))


def _softplus(x):
    return jnp.maximum(x, 0.0) + jnp.log1p(jnp.exp(-jnp.abs(x)))


def _rmsnorm_kernel(x_ref, g_ref, o_ref):
    x = x_ref[...]
    ms = jnp.mean(x * x, axis=-1, keepdims=True)
    o_ref[...] = (x * lax.rsqrt(ms + RMS_EPS) * g_ref[...]).astype(o_ref.dtype)


def rmsnorm(x, g, out_dtype, tm=512):
    t, d = x.shape
    return pl.pallas_call(
        _rmsnorm_kernel,
        out_shape=jax.ShapeDtypeStruct((t, d), out_dtype),
        grid=(t // tm,),
        in_specs=[pl.BlockSpec((tm, d), lambda i: (i, 0)),
                  pl.BlockSpec((1, d), lambda i: (0, 0))],
        out_specs=pl.BlockSpec((tm, d), lambda i: (i, 0)),
        compiler_params=_params(("parallel",)),
        name="rmsnorm",
    )(x, g.reshape(1, d))


def _mm_act_kernel(a_ref, w_ref, o_ref, wb_ref, *, act):
    @pl.when(pl.program_id(1) == 0)
    def _():
        wb_ref[...] = w_ref[...].astype(BF16)

    y = jnp.dot(a_ref[...], wb_ref[...], preferred_element_type=F32)
    if act == "relu2":
        y = jnp.square(jnp.maximum(y, 0.0))
    elif act == "gelu":
        y = _gelu_tanh(y)
    o_ref[...] = y.astype(o_ref.dtype)


def mm_act(a, w, act, out_dtype, col0=0, n=None, tm=1024, tn=1024):
    t, k = a.shape
    n = w.shape[1] if n is None else n
    j0 = col0 // tn
    return pl.pallas_call(
        functools.partial(_mm_act_kernel, act=act),
        out_shape=jax.ShapeDtypeStruct((t, n), out_dtype),
        grid=(n // tn, t // tm),
        in_specs=[pl.BlockSpec((tm, k), lambda j, i: (i, 0)),
                  pl.BlockSpec((k, tn), lambda j, i: (0, j0 + j))],
        out_specs=pl.BlockSpec((tm, tn), lambda j, i: (i, j)),
        scratch_shapes=[pltpu.VMEM((k, tn), BF16)],
        compiler_params=_params(("parallel", "arbitrary")),
        name="mm_" + act,
    )(a, w)


def _mm_res_norm_kernel(a_ref, w_ref, h_ref, g_ref, *refs, emit_h, nk):
    refs = list(refs)
    hnew_ref = refs.pop(0) if emit_h else None
    hn_ref = refs.pop(0)
    kk = pl.program_id(1)

    def part():
        return jnp.dot(a_ref[...], w_ref[...], preferred_element_type=F32)

    def finish(acc):
        hnew = h_ref[...] + acc
        if emit_h:
            hnew_ref[...] = hnew
        ms = jnp.mean(hnew * hnew, axis=-1, keepdims=True)
        hn_ref[...] = (hnew * lax.rsqrt(ms + RMS_EPS) * g_ref[...]).astype(hn_ref.dtype)

    if nk == 1:
        finish(part())
        return
    acc_ref = refs.pop(0)

    @pl.when(kk == 0)
    def _():
        acc_ref[...] = part()

    @pl.when(kk > 0)
    def _():
        acc_ref[...] += part()

    @pl.when(kk == nk - 1)
    def _():
        finish(acc_ref[...])


def mm_res_norm(a, w, h, g, hn_dtype, emit_h=True, tm=512, tk=1024):
    t, k = a.shape
    d = w.shape[1]
    nk = k // tk
    row_spec = pl.BlockSpec((tm, d), lambda i, kk: (i, 0))
    out_shape = [jax.ShapeDtypeStruct((t, d), hn_dtype)]
    out_specs = [row_spec]
    if emit_h:
        out_shape = [jax.ShapeDtypeStruct((t, d), F32)] + out_shape
        out_specs = [row_spec] + out_specs
    res = pl.pallas_call(
        functools.partial(_mm_res_norm_kernel, emit_h=emit_h, nk=nk),
        out_shape=out_shape,
        grid=(t // tm, nk),
        in_specs=[pl.BlockSpec((tm, tk), lambda i, kk: (i, kk)),
                  pl.BlockSpec((tk, d), lambda i, kk: (kk, 0)),
                  row_spec,
                  pl.BlockSpec((1, d), lambda i, kk: (0, 0))],
        out_specs=out_specs,
        scratch_shapes=[pltpu.VMEM((tm, d), F32)] if nk > 1 else [],
        compiler_params=_params(("parallel", "arbitrary")),
        name="mm_res_norm",
    )(a, w, h, g.reshape(1, d))
    if emit_h:
        return res[0], res[1]
    return None, res[0]


def _rg_scan_kernel(rec_ref, gate_ref, cw_ref, cb_ref, gxw_ref, gxb_ref, gaw_ref, gab_ref,
                    lam_ref, y_ref, tail_ref, hc_ref, *, tt):
    c = pl.program_id(2)

    @pl.when(c == 0)
    def _():
        tail_ref[...] = jnp.zeros_like(tail_ref)
        hc_ref[...] = jnp.zeros_like(hc_ref)

    x = rec_ref[...]
    tail = tail_ref[...]
    row8 = lax.broadcasted_iota(jnp.int32, tail.shape, 0)
    xc = cb_ref[...] + cw_ref[CONV_W - 1:CONV_W, :] * x
    for j in range(1, CONV_W):
        xr = pltpu.roll(x, j, 0)
        tr = pltpu.roll(tail, j, 0)
        head = jnp.where(row8 < j, tr, xr[:SUBLANES])
        xs = jnp.concatenate([head, xr[SUBLANES:]], axis=0)
        xc = xc + cw_ref[CONV_W - 1 - j:CONV_W - j, :] * xs
    tail_ref[...] = x[tt - SUBLANES:]

    xb = xc.astype(BF16)
    i_t = jax.nn.sigmoid(jnp.dot(xb, gxw_ref[...], preferred_element_type=F32) + gxb_ref[...])
    r_t = jax.nn.sigmoid(jnp.dot(xb, gaw_ref[...], preferred_element_type=F32) + gab_ref[...])
    log_sig_lam = -_softplus(-lam_ref[...])
    log_a = RG_C * r_t * log_sig_lam
    a = jnp.exp(log_a)
    th = jnp.tanh(log_a)
    mult = jnp.sqrt(jnp.maximum(-2.0 * th / (1.0 - th), 0.0))
    b = mult * (i_t * xc)

    row = lax.broadcasted_iota(jnp.int32, x.shape, 0)
    s = 1
    while s < SUBLANES:
        valid = row >= s
        b = jnp.where(valid, a * pltpu.roll(b, s, 0) + b, b)
        a = jnp.where(valid, a * pltpu.roll(a, s, 0), a)
        s *= 2
    while s < tt:
        b = jnp.concatenate([b[:s], a[s:] * b[:tt - s] + b[s:]], axis=0)
        a = jnp.concatenate([a[:s], a[s:] * a[:tt - s]], axis=0)
        s *= 2
    h = a * hc_ref[...] + b
    hc_ref[...] = h[tt - 1:tt]
    y_ref[...] = (h * gate_ref[...]).astype(y_ref.dtype)


def rg_scan(rec, gate, conv_w, conv_b, gx_w, gx_b, ga_w, ga_b, lam, batch, tt=256):
    t, d = rec.shape
    seq = t // batch
    nc = seq // tt
    blk = d // RG_HEADS
    act_spec = pl.BlockSpec((tt, blk), lambda b, h, c: (b * nc + c, h))
    vec_spec = pl.BlockSpec((1, blk), lambda b, h, c: (0, h))
    w_spec = pl.BlockSpec((None, blk, blk), lambda b, h, c: (h, 0, 0))
    return pl.pallas_call(
        functools.partial(_rg_scan_kernel, tt=tt),
        out_shape=jax.ShapeDtypeStruct((t, d), BF16),
        grid=(batch, RG_HEADS, nc),
        in_specs=[act_spec, act_spec,
                  pl.BlockSpec((CONV_W, blk), lambda b, h, c: (0, h)), vec_spec,
                  w_spec, vec_spec, w_spec, vec_spec, vec_spec],
        out_specs=act_spec,
        scratch_shapes=[pltpu.VMEM((SUBLANES, blk), F32), pltpu.VMEM((1, blk), F32)],
        compiler_params=_params(("parallel", "parallel", "arbitrary")),
        name="rg_scan",
    )(rec, gate, conv_w, conv_b.reshape(1, d), gx_w, gx_b.reshape(1, d),
      ga_w, ga_b.reshape(1, d), lam.reshape(1, d))


def _token_shift_delta(x, prev_ref, is_first):
    prev_last = jnp.where(is_first, 0.0, prev_ref[SUBLANES - 1:SUBLANES, :])
    row = lax.broadcasted_iota(jnp.int32, x.shape, 0)
    xs = jnp.where(row == 0, prev_last, pltpu.roll(x, 1, 0))
    return xs - x


def _rw_proj_kernel(x_ref, prev_ref, mu_ref, w_ref, o_ref, *, blocks_per_seq):
    i = pl.program_id(1)
    x = x_ref[...]
    xx = _token_shift_delta(x, prev_ref, (i % blocks_per_seq) == 0)
    xm = (x + xx * mu_ref[...]).astype(BF16)
    o_ref[...] = jnp.dot(xm, w_ref[...], preferred_element_type=F32)


def rw_proj(hn, mu, w_rkv, seq, tm=512):
    t, d = hn.shape
    nmat = w_rkv.shape[0]
    return pl.pallas_call(
        functools.partial(_rw_proj_kernel, blocks_per_seq=seq // tm),
        out_shape=jax.ShapeDtypeStruct((nmat, t, d), F32),
        grid=(nmat, t // tm),
        in_specs=[pl.BlockSpec((tm, d), lambda j, i: (i, 0)),
                  pl.BlockSpec((SUBLANES, d),
                               lambda j, i: (jnp.maximum(i * (tm // SUBLANES) - 1, 0), 0)),
                  pl.BlockSpec((None, 1, d), lambda j, i: (j, 0, 0)),
                  pl.BlockSpec((None, d, d), lambda j, i: (j, 0, 0))],
        out_specs=pl.BlockSpec((None, tm, d), lambda j, i: (j, i, 0)),
        compiler_params=_params(("parallel", "parallel")),
        name="rw_proj",
    )(hn, hn, mu.reshape(nmat, 1, d), w_rkv)


def _rw_lora_kernel(x_ref, prev_ref, mu_ref, w0_ref, w1_ref, w2_ref, a0_ref, a1_ref, a2_ref,
                    g1_ref, g2_ref, lw_ref, a_ref, g_ref, *, blocks_per_seq):
    i = pl.program_id(0)
    x = x_ref[...]
    xx = _token_shift_delta(x, prev_ref, (i % blocks_per_seq) == 0)
    xw = (x + xx * mu_ref[0:1, :]).astype(BF16)
    xa = (x + xx * mu_ref[1:2, :]).astype(BF16)
    xg = (x + xx * mu_ref[2:3, :]).astype(BF16)
    hid = jnp.tanh(jnp.dot(xw, w1_ref[...], preferred_element_type=F32))
    w = w0_ref[...] + _dot(hid, w2_ref[...])
    w = -_softplus(-w) - 0.5
    lw_ref[...] = -jnp.exp(w)
    hid = jnp.dot(xa, a1_ref[...], preferred_element_type=F32)
    a_ref[...] = jax.nn.sigmoid(a0_ref[...] + _dot(hid, a2_ref[...]))
    hid = jax.nn.sigmoid(jnp.dot(xg, g1_ref[...], preferred_element_type=F32))
    g_ref[...] = _dot(hid, g2_ref[...])


def rw_lora(hn, mu3, w0, w1, w2, a0, a1, a2, g1, g2, seq, tm=512):
    t, d = hn.shape
    row_spec = pl.BlockSpec((tm, d), lambda i: (i, 0))

    def full(arr):
        return pl.BlockSpec(arr.shape, lambda i: (0,) * arr.ndim)

    w0, a0 = w0.reshape(1, d), a0.reshape(1, d)
    consts = (mu3, w0, w1, w2, a0, a1, a2, g1, g2)
    return pl.pallas_call(
        functools.partial(_rw_lora_kernel, blocks_per_seq=seq // tm),
        out_shape=[jax.ShapeDtypeStruct((t, d), F32)] * 3,
        grid=(t // tm,),
        in_specs=[row_spec,
                  pl.BlockSpec((SUBLANES, d),
                               lambda i: (jnp.maximum(i * (tm // SUBLANES) - 1, 0), 0))]
                 + [full(c) for c in consts],
        out_specs=[row_spec] * 3,
        compiler_params=_params(("parallel",)),
        name="rw_lora",
    )(hn, hn, *consts)


def _split_terms(x, passes):
    terms = []
    for i in range(passes):
        part = x.astype(BF16)
        terms.append(part)
        if i + 1 < passes:
            x = x - part.astype(F32)
    return terms


def _unit_lower_inverse_many(lps, row, col, eye, mul):
    diag8 = jnp.right_shift(row, 3) == jnp.right_shift(col, 3)
    lds = [jnp.where(diag8, lp, 0.0) for lp in lps]
    l2s = [mul(ld, ld) for ld in lds]
    l4s = [mul(l2, l2) for l2 in l2s]
    ts = [mul(eye - ld, eye + l2) for ld, l2 in zip(lds, l2s)]
    ts = [mul(t, eye + l4) for t, l4 in zip(ts, l4s)]
    for sh in (3, 4, 5):
        rb, cb = jnp.right_shift(row, sh), jnp.right_shift(col, sh)
        sel = (jnp.right_shift(rb, 1) == jnp.right_shift(cb, 1)) & (rb != cb)
        ots = [mul(jnp.where(sel, lp, 0.0), t) for lp, t in zip(lps, ts)]
        ts = [t - mul(t, ot) for t, ot in zip(ts, ots)]
    return ts


def _rw_rec_kernel(r_ref, k_ref, v_ref, lw_ref, a_ref, g_ref, kk_ref, ka_ref, rk_ref,
                   lng_ref, lnb_ref, o_ref, s_ref, y_ref, *, n_chunks, n_pairs):
    c = pl.program_id(2)

    @pl.when(c == 0)
    def _():
        s_ref[...] = jnp.zeros_like(s_ref)

    tt, lwid = n_chunks * CHUNK, n_pairs * LANES
    n2 = 2 * CHUNK
    shift = CHUNK.bit_length() - 1
    row = lax.broadcasted_iota(jnp.int32, (CHUNK, n2), 0)
    col = jnp.bitwise_and(lax.broadcasted_iota(jnp.int32, (CHUNK, n2), 1), CHUNK - 1)
    strict = row > col
    incl = row >= col
    eye = (row == col).astype(F32)
    trow = lax.broadcasted_iota(jnp.int32, (tt, tt), 0)
    tcol = lax.broadcasted_iota(jnp.int32, (tt, tt), 1)
    tri = ((jnp.right_shift(trow, shift) == jnp.right_shift(tcol, shift))
           & (trow >= tcol)).astype(BF16)
    hrow = lax.broadcasted_iota(jnp.int32, (lwid, lwid), 0)
    hcol = lax.broadcasted_iota(jnp.int32, (lwid, lwid), 1)
    head_ones = (jnp.right_shift(hrow, shift) == jnp.right_shift(hcol, shift)).astype(BF16)
    lane = lax.broadcasted_iota(jnp.int32, (CHUNK, n2), 1)
    m0 = (lane < RW_HEAD).astype(F32)
    m1 = 1.0 - m0

    def head_sum(x, passes=2):
        return sum(jnp.dot(p, head_ones, preferred_element_type=F32)
                   for p in _split_terms(x, passes))

    def stack(x):
        return jnp.concatenate([x * m0, x * m1], axis=0)

    r, k, v = r_ref[...], k_ref[...], v_ref[...]
    lw, a = lw_ref[...], a_ref[...]
    kkp = k * kk_ref[...]
    kap = kkp / jnp.maximum(jnp.sqrt(head_sum(kkp * kkp)), 1e-12)
    b = kap * a
    km = k * (1.0 + (a - 1.0) * ka_ref[...])
    cum = sum(jnp.dot(tri, p, preferred_element_type=F32) for p in _split_terms(lw, 3))
    e_in = jnp.exp(cum)
    e_neg = jnp.exp(-cum)
    kt = kap * jnp.exp(cum - lw)
    bt = b * e_neg
    kmt = km * e_neg
    rt = r * e_in

    items = [(ci, p) for ci in range(n_chunks) for p in range(n_pairs)]

    def blk(x, ci, p):
        return x[CHUNK * ci:CHUNK * (ci + 1), LANES * p:LANES * (p + 1)]

    def mul(a_sb, b_sb):
        return jnp.dot(a_sb.astype(BF16), stack(b_sb).astype(BF16), preferred_element_type=F32)

    ktn = [blk(kt, *it) for it in items]
    rtn = [blk(rt, *it) for it in items]
    kts = [stack(x).astype(BF16) for x in ktn]
    bts = [stack(blk(bt, *it)).astype(BF16) for it in items]
    kms = [stack(blk(kmt, *it)).astype(BF16) for it in items]
    vs = [stack(blk(v, *it)) for it in items]
    vsb = [x.astype(BF16) for x in vs]
    pcs = [blk(e_in, *it)[CHUNK - 1:CHUNK, :] for it in items]
    kbs = [jnp.concatenate([km_, bt_], axis=0) for km_, bt_ in zip(kms, bts)]
    amats = [_dot_nt(jnp.concatenate([kt_, rt_], axis=0), kb)
             for kt_, rt_, kb in zip(ktn, rtn, kbs)]
    lps = [jnp.where(strict, am[:CHUNK, n2:], 0.0) for am in amats]
    mps = [jnp.where(strict, am[:CHUNK, :n2], 0.0).astype(BF16) for am in amats]
    arbs = [jnp.where(incl, am[CHUNK:, n2:], 0.0).astype(BF16) for am in amats]
    arks = [jnp.where(incl, am[CHUNK:, :n2], 0.0).astype(BF16) for am in amats]
    tps = [t.astype(BF16) for t in _unit_lower_inverse_many(lps, row, col, eye, mul)]
    mvs = [jnp.dot(mp, v_, preferred_element_type=F32) for mp, v_ in zip(mps, vsb)]
    wus = [jnp.dot(tp, jnp.concatenate([kt_, stack(mv).astype(BF16)], axis=1),
                   preferred_element_type=F32)
           for tp, kt_, mv in zip(tps, kts, mvs)]
    wss = [stack(wu[:, :n2]) for wu in wus]
    uss = [stack(wu[:, n2:]) for wu in wus]
    awus = [jnp.dot(arb, jnp.concatenate([ws, us], axis=1).astype(BF16), preferred_element_type=F32)
            for arb, ws, us in zip(arbs, wss, uss)]
    qs = [(rt_ - awu[:, :n2]).astype(BF16) for rt_, awu in zip(rtn, awus)]
    ylocs = [jnp.dot(ark, v_, preferred_element_type=F32) - awu[:, n2:]
             for ark, v_, awu in zip(arks, vsb, awus)]
    gms = [(-_dot(ws.T, bt_) * pc).astype(BF16) for ws, bt_, pc in zip(wss, bts, pcs)]
    hms = [_dot(jnp.concatenate([v_, -us], axis=0).T, kb) * pc
           for v_, us, kb, pc in zip(vs, uss, kbs, pcs)]

    states = [s_ref[p] for p in range(n_pairs)]
    for idx, (ci, p) in enumerate(items):
        sb = states[p].astype(BF16)
        y_ref[CHUNK * ci:CHUNK * (ci + 1), LANES * p:LANES * (p + 1)] = (
            _dot_nt(qs[idx], sb) + ylocs[idx])
        states[p] = (states[p] * pcs[idx] + jnp.dot(sb, gms[idx], preferred_element_type=F32)
                     + hms[idx])
    for p in range(n_pairs):
        s_ref[p] = states[p]

    y = y_ref[...]
    dlt = y - head_sum(y) * (1.0 / RW_HEAD)
    var = head_sum(dlt * dlt) * (1.0 / RW_HEAD)
    yn = dlt * lax.rsqrt(var + GN_EPS) * lng_ref[...] + lnb_ref[...]
    bonus = head_sum(r * km * rk_ref[...]) * v
    o_ref[...] = ((yn + bonus) * g_ref[...]).astype(o_ref.dtype)


def rw_rec(rkv, lw, a, g, k_k, k_a, r_k, ln_g, ln_b, batch, n_chunks=4, n_pairs=4):
    _, t, d = rkv.shape
    seq = t // batch
    tt = n_chunks * CHUNK
    lwid = n_pairs * LANES
    nc = seq // tt

    def rkv_spec(m):
        return pl.BlockSpec((None, tt, lwid), lambda b, l, c: (m, b * nc + c, l))

    act_spec = pl.BlockSpec((tt, lwid), lambda b, l, c: (b * nc + c, l))
    vec_spec = pl.BlockSpec((1, lwid), lambda b, l, c: (0, l))
    vecs = [x.reshape(1, d) for x in (k_k, k_a, r_k, ln_g, ln_b)]
    return pl.pallas_call(
        functools.partial(_rw_rec_kernel, n_chunks=n_chunks, n_pairs=n_pairs),
        out_shape=jax.ShapeDtypeStruct((t, d), BF16),
        grid=(batch, d // lwid, nc),
        in_specs=[rkv_spec(0), rkv_spec(1), rkv_spec(2), act_spec, act_spec, act_spec]
                 + [vec_spec] * 5,
        out_specs=act_spec,
        scratch_shapes=[pltpu.VMEM((n_pairs, LANES, LANES), F32), pltpu.VMEM((tt, lwid), F32)],
        compiler_params=_params(("parallel", "parallel", "arbitrary")),
        name="rw_rec",
    )(rkv, rkv, rkv, lw, a, g, *vecs)


def _pad_axis(x, axis, size):
    pad = [(0, 0)] * x.ndim
    pad[axis] = (0, size - x.shape[axis])
    return jnp.pad(x, pad)


def kernel(x, norm_mix_g, norm_mlp_g, w_mlp_up, w_mlp_down, final_norm_g, rg_w_in, rg_conv_w, rg_conv_b, rg_gx_w, rg_gx_b, rg_ga_w, rg_ga_b, rg_lambda, rg_w_out, rw_mu, rw_w_rkv, rw_w0, rw_w1, rw_w2, rw_a0, rw_a1, rw_a2, rw_g1, rw_g2, rw_k_k, rw_k_a, rw_r_k, rw_ln_g, rw_ln_b, rw_w_out):
    batch, seq, d = x.shape
    depth = norm_mix_g.shape[0]
    t = batch * seq
    h = x.reshape(t, d)
    hn = rmsnorm(h, norm_mix_g[0], BF16)
    for i in range(depth):
        j = i // 2
        if i % 2 == 0:
            gate = mm_act(hn, rg_w_in[j], "gelu", F32, col0=0, n=d)
            rec = mm_act(hn, rg_w_in[j], "none", F32, col0=d, n=d)
            y = rg_scan(rec, gate, rg_conv_w[j], rg_conv_b[j],
                        rg_gx_w[j].astype(BF16), rg_gx_b[j],
                        rg_ga_w[j].astype(BF16), rg_ga_b[j], rg_lambda[j], batch)
            w_out = rg_w_out[j].astype(BF16)
        else:
            rkv = rw_proj(hn, rw_mu[j, :3], rw_w_rkv[j].astype(BF16), seq)
            lora = LANES
            lw, a, g = rw_lora(
                hn, rw_mu[j, 3:], rw_w0[j],
                _pad_axis(rw_w1[j], 1, lora).astype(BF16), _pad_axis(rw_w2[j], 0, lora).astype(BF16),
                rw_a0[j],
                _pad_axis(rw_a1[j], 1, lora).astype(BF16), _pad_axis(rw_a2[j], 0, lora).astype(BF16),
                rw_g1[j].astype(BF16), rw_g2[j].astype(BF16), seq)
            y = rw_rec(rkv, lw, a, g, rw_k_k[j], rw_k_a[j], rw_r_k[j].reshape(d),
                       rw_ln_g[j], rw_ln_b[j], batch)
            w_out = rw_w_out[j].astype(BF16)
        h, hn = mm_res_norm(y, w_out, h, norm_mlp_g[i], BF16, tk=d)
        u = mm_act(hn, w_mlp_up[i], "relu2", BF16)
        last = i + 1 == depth
        g_next = final_norm_g if last else norm_mix_g[i + 1]
        hn_dtype = F32 if (last or (i + 1) % 2 == 1) else BF16
        h, hn = mm_res_norm(u, w_mlp_down[i].astype(BF16), h, g_next, hn_dtype, emit_h=not last)
    return hn.reshape(batch, seq, d)
```

```python
import functools

import jax
import jax.numpy as jnp
from jax import lax
from jax.experimental import pallas as pl
from jax.experimental.pallas import tpu as pltpu

F32 = jnp.float32
BF16 = jnp.bfloat16

RMS_EPS = 1e-6
GN_EPS = 64e-5
RG_C = 8.0
RG_HEADS = 8
CONV_W = 4
RW_HEAD = 64
LANES = 128
MXU_DIM = 256
SUBLANES = 8
CHUNK = 64
VMEM_LIMIT = 56 * 1024 * 1024


def _params(sem):
    return pltpu.CompilerParams(dimension_semantics=sem, vmem_limit_bytes=VMEM_LIMIT)


def _dot(a, b):
    return jnp.dot(a.astype(BF16), b.astype(BF16), preferred_element_type=F32)


def _dot_nt(a, b):
    return lax.dot_general(a.astype(BF16), b.astype(BF16), (((1,), (1,)), ((), ())),
                           preferred_element_type=F32)


def _gelu_tanh(x):
    return 0.5 * x * (1.0 + jnp.tanh(0.7978845608028654 * (x + 0.044715 * (x * x * x))))


def _softplus(x):
    return jnp.maximum(x, 0.0) + jnp.log1p(jnp.exp(-jnp.abs(x)))


def _rmsnorm_kernel(x_ref, g_ref, o_ref):
    x = x_ref[...]
    ms = jnp.mean(x * x, axis=-1, keepdims=True)
    o_ref[...] = (x * lax.rsqrt(ms + RMS_EPS) * g_ref[...]).astype(o_ref.dtype)


def rmsnorm(x, g, out_dtype, tm=512):
    t, d = x.shape
    return pl.pallas_call(
        _rmsnorm_kernel,
        out_shape=jax.ShapeDtypeStruct((t, d), out_dtype),
        grid=(t // tm,),
        in_specs=[pl.BlockSpec((tm, d), lambda i: (i, 0)),
                  pl.BlockSpec((1, d), lambda i: (0, 0))],
        out_specs=pl.BlockSpec((tm, d), lambda i: (i, 0)),
        compiler_params=_params(("parallel",)),
        name="rmsnorm",
    )(x, g.reshape(1, d))


def _mm_act_kernel(a_ref, w_ref, o_ref, wb_ref, *, act):
    @pl.when(pl.program_id(1) == 0)
    def _():
        wb_ref[...] = w_ref[...].astype(BF16)

    y = jnp.dot(a_ref[...], wb_ref[...], preferred_element_type=F32)
    if act == "relu2":
        y = jnp.square(jnp.maximum(y, 0.0))
    elif act == "gelu":
        y = _gelu_tanh(y)
    o_ref[...] = y.astype(o_ref.dtype)


def mm_act(a, w, layer, act, out_dtype, col0=0, n=None, tm=1024, tn=1024):
    t, k = a.shape
    n = w.shape[2] if n is None else n
    j0 = col0 // tn
    return pl.pallas_call(
        functools.partial(_mm_act_kernel, act=act),
        out_shape=jax.ShapeDtypeStruct((t, n), out_dtype),
        grid=(n // tn, t // tm),
        in_specs=[pl.BlockSpec((tm, k), lambda j, i: (i, 0)),
                  pl.BlockSpec((None, k, tn), lambda j, i: (layer, 0, j0 + j))],
        out_specs=pl.BlockSpec((tm, tn), lambda j, i: (i, j)),
        scratch_shapes=[pltpu.VMEM((k, tn), BF16)],
        compiler_params=_params(("parallel", "arbitrary")),
        name="mm_" + act,
    )(a, w)


def _mm_res_norm_kernel(a_ref, w_ref, h_ref, g_ref, *refs, emit_h, nk):
    refs = list(refs)
    hnew_ref = refs.pop(0) if emit_h else None
    hn_ref = refs.pop(0)
    kk = pl.program_id(1)

    def part():
        return jnp.dot(a_ref[...], w_ref[...], preferred_element_type=F32)

    def finish(acc):
        hnew = h_ref[...] + acc
        if emit_h:
            hnew_ref[...] = hnew
        ms = jnp.mean(hnew * hnew, axis=-1, keepdims=True)
        hn_ref[...] = (hnew * lax.rsqrt(ms + RMS_EPS) * g_ref[...]).astype(hn_ref.dtype)

    if nk == 1:
        finish(part())
        return
    acc_ref = refs.pop(0)

    @pl.when(kk == 0)
    def _():
        acc_ref[...] = part()

    @pl.when(kk > 0)
    def _():
        acc_ref[...] += part()

    @pl.when(kk == nk - 1)
    def _():
        finish(acc_ref[...])


def mm_res_norm(a, w, layer, h, g, hn_dtype, emit_h=True, tm=512, tk=1024):
    t, k = a.shape
    d = w.shape[2]
    nk = k // tk
    row_spec = pl.BlockSpec((tm, d), lambda i, kk: (i, 0))
    out_shape = [jax.ShapeDtypeStruct((t, d), hn_dtype)]
    out_specs = [row_spec]
    if emit_h:
        out_shape = [jax.ShapeDtypeStruct((t, d), F32)] + out_shape
        out_specs = [row_spec] + out_specs
    res = pl.pallas_call(
        functools.partial(_mm_res_norm_kernel, emit_h=emit_h, nk=nk),
        out_shape=out_shape,
        grid=(t // tm, nk),
        in_specs=[pl.BlockSpec((tm, tk), lambda i, kk: (i, kk)),
                  pl.BlockSpec((None, tk, d), lambda i, kk: (layer, kk, 0)),
                  row_spec,
                  pl.BlockSpec((1, d), lambda i, kk: (0, 0))],
        out_specs=out_specs,
        scratch_shapes=[pltpu.VMEM((tm, d), F32)] if nk > 1 else [],
        compiler_params=_params(("parallel", "arbitrary")),
        name="mm_res_norm",
    )(a, w, h, g.reshape(1, d))
    if emit_h:
        return res[0], res[1]
    return None, res[0]


def _rg_scan_kernel(rec_ref, gate_ref, cw_ref, cb_ref, gxw_ref, gxb_ref, gaw_ref, gab_ref,
                    lam_ref, y_ref, tail_ref, hc_ref, *, tt):
    c = pl.program_id(2)

    @pl.when(c == 0)
    def _():
        tail_ref[...] = jnp.zeros_like(tail_ref)
        hc_ref[...] = jnp.zeros_like(hc_ref)

    x = rec_ref[...]
    tail = tail_ref[...]
    row8 = lax.broadcasted_iota(jnp.int32, tail.shape, 0)
    xc = cb_ref[...] + cw_ref[CONV_W - 1:CONV_W, :] * x
    for j in range(1, CONV_W):
        xr = pltpu.roll(x, j, 0)
        tr = pltpu.roll(tail, j, 0)
        head = jnp.where(row8 < j, tr, xr[:SUBLANES])
        xs = jnp.concatenate([head, xr[SUBLANES:]], axis=0)
        xc = xc + cw_ref[CONV_W - 1 - j:CONV_W - j, :] * xs
    tail_ref[...] = x[tt - SUBLANES:]

    xb = xc.astype(BF16)
    i_t = jax.nn.sigmoid(jnp.dot(xb, gxw_ref[...], preferred_element_type=F32) + gxb_ref[...])
    r_t = jax.nn.sigmoid(jnp.dot(xb, gaw_ref[...], preferred_element_type=F32) + gab_ref[...])
    log_sig_lam = -_softplus(-lam_ref[...])
    log_a = RG_C * r_t * log_sig_lam
    a = jnp.exp(log_a)
    th = jnp.tanh(log_a)
    mult = jnp.sqrt(jnp.maximum(-2.0 * th / (1.0 - th), 0.0))
    b = mult * (i_t * xc)

    row = lax.broadcasted_iota(jnp.int32, x.shape, 0)
    s = 1
    while s < SUBLANES:
        valid = row >= s
        b = jnp.where(valid, a * pltpu.roll(b, s, 0) + b, b)
        a = jnp.where(valid, a * pltpu.roll(a, s, 0), a)
        s *= 2
    while s < tt:
        b = jnp.concatenate([b[:s], a[s:] * b[:tt - s] + b[s:]], axis=0)
        a = jnp.concatenate([a[:s], a[s:] * a[:tt - s]], axis=0)
        s *= 2
    h = a * hc_ref[...] + b
    hc_ref[...] = h[tt - 1:tt]
    y_ref[...] = (h * gate_ref[...]).astype(y_ref.dtype)


def rg_scan(rec, gate, conv_w, conv_b, gx_w, gx_b, ga_w, ga_b, lam, layer, batch, tt=256):
    t, d = rec.shape
    seq = t // batch
    nc = seq // tt
    blk = d // RG_HEADS
    act_spec = pl.BlockSpec((tt, blk), lambda b, h, c: (b * nc + c, h))
    vec_spec = pl.BlockSpec((1, blk), lambda b, h, c: (0, h))
    w_spec = pl.BlockSpec((None, None, blk, blk), lambda b, h, c: (layer, h, 0, 0))
    return pl.pallas_call(
        functools.partial(_rg_scan_kernel, tt=tt),
        out_shape=jax.ShapeDtypeStruct((t, d), BF16),
        grid=(batch, RG_HEADS, nc),
        in_specs=[act_spec, act_spec,
                  pl.BlockSpec((CONV_W, blk), lambda b, h, c: (0, h)), vec_spec,
                  w_spec, vec_spec, w_spec, vec_spec, vec_spec],
        out_specs=act_spec,
        scratch_shapes=[pltpu.VMEM((SUBLANES, blk), F32), pltpu.VMEM((1, blk), F32)],
        compiler_params=_params(("parallel", "parallel", "arbitrary")),
        name="rg_scan",
    )(rec, gate, conv_w, conv_b.reshape(1, d), gx_w, gx_b.reshape(1, d),
      ga_w, ga_b.reshape(1, d), lam.reshape(1, d))


def _token_shift_delta(x, prev_ref, is_first):
    prev_last = jnp.where(is_first, 0.0, prev_ref[SUBLANES - 1:SUBLANES, :])
    row = lax.broadcasted_iota(jnp.int32, x.shape, 0)
    xs = jnp.where(row == 0, prev_last, pltpu.roll(x, 1, 0))
    return xs - x


def _rw_proj_kernel(x_ref, prev_ref, mu_ref, w_ref, o_ref, *, blocks_per_seq):
    i = pl.program_id(1)
    x = x_ref[...]
    xx = _token_shift_delta(x, prev_ref, (i % blocks_per_seq) == 0)
    xm = (x + xx * mu_ref[...]).astype(BF16)
    o_ref[...] = jnp.dot(xm, w_ref[...], preferred_element_type=F32)


def rw_proj(hn, mu, w_rkv, layer, seq, tm=512):
    t, d = hn.shape
    nmat = w_rkv.shape[1]
    return pl.pallas_call(
        functools.partial(_rw_proj_kernel, blocks_per_seq=seq // tm),
        out_shape=jax.ShapeDtypeStruct((nmat, t, d), F32),
        grid=(nmat, t // tm),
        in_specs=[pl.BlockSpec((tm, d), lambda j, i: (i, 0)),
                  pl.BlockSpec((SUBLANES, d),
                               lambda j, i: (jnp.maximum(i * (tm // SUBLANES) - 1, 0), 0)),
                  pl.BlockSpec((None, 1, d), lambda j, i: (j, 0, 0)),
                  pl.BlockSpec((None, None, d, d), lambda j, i: (layer, j, 0, 0))],
        out_specs=pl.BlockSpec((None, tm, d), lambda j, i: (j, i, 0)),
        compiler_params=_params(("parallel", "parallel")),
        name="rw_proj",
    )(hn, hn, mu.reshape(nmat, 1, d), w_rkv)


def _rw_lora_kernel(x_ref, prev_ref, mu_ref, w0_ref, w1_ref, w2_ref, a0_ref, a1_ref, a2_ref,
                    g1_ref, g2_ref, lw_ref, a_ref, g_ref, *, blocks_per_seq):
    i = pl.program_id(0)
    x = x_ref[...]
    xx = _token_shift_delta(x, prev_ref, (i % blocks_per_seq) == 0)
    xw = (x + xx * mu_ref[0:1, :]).astype(BF16)
    xa = (x + xx * mu_ref[1:2, :]).astype(BF16)
    xg = (x + xx * mu_ref[2:3, :]).astype(BF16)
    hid = jnp.tanh(jnp.dot(xw, w1_ref[...], preferred_element_type=F32))
    w = w0_ref[...] + _dot(hid, w2_ref[...])
    w = -_softplus(-w) - 0.5
    lw_ref[...] = -jnp.exp(w)
    hid = jnp.dot(xa, a1_ref[...], preferred_element_type=F32)
    a_ref[...] = jax.nn.sigmoid(a0_ref[...] + _dot(hid, a2_ref[...]))
    hid = jax.nn.sigmoid(jnp.dot(xg, g1_ref[...], preferred_element_type=F32))
    g_ref[...] = _dot(hid, g2_ref[...])


def rw_lora(hn, mu3, w0, w1, w2, a0, a1, a2, g1, g2, seq, tm=512):
    t, d = hn.shape
    row_spec = pl.BlockSpec((tm, d), lambda i: (i, 0))

    def full(arr):
        return pl.BlockSpec(arr.shape, lambda i: (0,) * arr.ndim)

    w0, a0 = w0.reshape(1, d), a0.reshape(1, d)
    consts = (mu3, w0, w1, w2, a0, a1, a2, g1, g2)
    return pl.pallas_call(
        functools.partial(_rw_lora_kernel, blocks_per_seq=seq // tm),
        out_shape=[jax.ShapeDtypeStruct((t, d), F32)] * 3,
        grid=(t // tm,),
        in_specs=[row_spec,
                  pl.BlockSpec((SUBLANES, d),
                               lambda i: (jnp.maximum(i * (tm // SUBLANES) - 1, 0), 0))]
                 + [full(c) for c in consts],
        out_specs=[row_spec] * 3,
        compiler_params=_params(("parallel",)),
        name="rw_lora",
    )(hn, hn, *consts)


def _split_terms(x, passes):
    terms = []
    for i in range(passes):
        part = x.astype(BF16)
        terms.append(part)
        if i + 1 < passes:
            x = x - part.astype(F32)
    return terms


def _unit_lower_inverse_many(lps, row, col, eye, mul):
    diag8 = jnp.right_shift(row, 3) == jnp.right_shift(col, 3)
    lds = [jnp.where(diag8, lp, 0.0) for lp in lps]
    l2s = [mul(ld, ld) for ld in lds]
    l4s = [mul(l2, l2) for l2 in l2s]
    ts = [mul(eye - ld, eye + l2) for ld, l2 in zip(lds, l2s)]
    ts = [mul(t, eye + l4) for t, l4 in zip(ts, l4s)]
    for sh in (3, 4, 5):
        rb, cb = jnp.right_shift(row, sh), jnp.right_shift(col, sh)
        sel = (jnp.right_shift(rb, 1) == jnp.right_shift(cb, 1)) & (rb != cb)
        ots = [mul(jnp.where(sel, lp, 0.0), t) for lp, t in zip(lps, ts)]
        ts = [t - mul(t, ot) for t, ot in zip(ts, ots)]
    return ts


def _rw_rec_kernel(r_ref, k_ref, v_ref, lw_ref, a_ref, g_ref, kk_ref, ka_ref, rk_ref,
                   lng_ref, lnb_ref, o_ref, s_ref, y_ref, *, n_chunks, n_pairs):
    c = pl.program_id(2)

    @pl.when(c == 0)
    def _():
        s_ref[...] = jnp.zeros_like(s_ref)

    tt, lwid = n_chunks * CHUNK, n_pairs * LANES
    n2 = 2 * CHUNK
    shift = CHUNK.bit_length() - 1
    row = lax.broadcasted_iota(jnp.int32, (CHUNK, n2), 0)
    col = jnp.bitwise_and(lax.broadcasted_iota(jnp.int32, (CHUNK, n2), 1), CHUNK - 1)
    strict = row > col
    incl = row >= col
    eye = (row == col).astype(F32)
    trow = lax.broadcasted_iota(jnp.int32, (tt, tt), 0)
    tcol = lax.broadcasted_iota(jnp.int32, (tt, tt), 1)
    tri = ((jnp.right_shift(trow, shift) == jnp.right_shift(tcol, shift))
           & (trow >= tcol)).astype(BF16)
    hw = min(lwid, MXU_DIM)
    hrow = lax.broadcasted_iota(jnp.int32, (hw, hw), 0)
    hcol = lax.broadcasted_iota(jnp.int32, (hw, hw), 1)
    head_ones = (jnp.right_shift(hrow, shift) == jnp.right_shift(hcol, shift)).astype(BF16)
    lane = lax.broadcasted_iota(jnp.int32, (CHUNK, n2), 1)
    m0 = (lane < RW_HEAD).astype(F32)
    m1 = 1.0 - m0
    m0b, m1b = m0.astype(BF16), m1.astype(BF16)

    def head_sum(x, passes):
        cols = []
        for c0 in range(0, lwid, hw):
            cols.append(sum(jnp.dot(p, head_ones, preferred_element_type=F32)
                            for p in _split_terms(x[:, c0:c0 + hw], passes)))
        return jnp.concatenate(cols, axis=1)

    def stack(x):
        return jnp.concatenate([x * m0, x * m1], axis=0)

    def stack_b(x):
        xb = x.astype(BF16)
        return jnp.concatenate([xb * m0b, xb * m1b], axis=0)

    r, k, v = r_ref[...], k_ref[...], v_ref[...]
    lw, a = lw_ref[...], a_ref[...]
    kkp = k * kk_ref[...]
    kap = kkp * lax.rsqrt(jnp.maximum(head_sum(kkp * kkp, 1), 1e-24))
    b = kap * a
    km = k * (1.0 + (a - 1.0) * ka_ref[...])
    cum = sum(jnp.dot(tri, p, preferred_element_type=F32) for p in _split_terms(lw, 3))
    e_in = jnp.exp(cum)
    e_neg = jnp.exp(-cum)
    kt = kap * jnp.exp(cum - lw)
    bt = b * e_neg
    kmt = km * e_neg
    rt = r * e_in

    items = [(ci, p) for ci in range(n_chunks) for p in range(n_pairs)]

    def blk(x, ci, p):
        return x[CHUNK * ci:CHUNK * (ci + 1), LANES * p:LANES * (p + 1)]

    def mul(a_sb, b_sb):
        return jnp.dot(a_sb.astype(BF16), stack_b(b_sb), preferred_element_type=F32)

    ktn = [blk(kt, *it) for it in items]
    rtn = [blk(rt, *it) for it in items]
    kts = [stack_b(x) for x in ktn]
    bts = [stack_b(blk(bt, *it)) for it in items]
    kms = [stack_b(blk(kmt, *it)) for it in items]
    vs = [stack(blk(v, *it)) for it in items]
    vsb = [x.astype(BF16) for x in vs]
    pcs = [blk(e_in, *it)[CHUNK - 1:CHUNK, :] for it in items]
    kbs = [jnp.concatenate([km_, bt_], axis=0) for km_, bt_ in zip(kms, bts)]
    amats = [_dot_nt(jnp.concatenate([kt_, rt_], axis=0), kb)
             for kt_, rt_, kb in zip(ktn, rtn, kbs)]
    lps = [jnp.where(strict, am[:CHUNK, n2:], 0.0) for am in amats]
    mps = [jnp.where(strict, am[:CHUNK, :n2], 0.0).astype(BF16) for am in amats]
    arbs = [jnp.where(incl, am[CHUNK:, n2:], 0.0).astype(BF16) for am in amats]
    arks = [jnp.where(incl, am[CHUNK:, :n2], 0.0).astype(BF16) for am in amats]
    tps = [t.astype(BF16) for t in _unit_lower_inverse_many(lps, row, col, eye, mul)]
    mvs = [jnp.dot(mp, v_, preferred_element_type=F32) for mp, v_ in zip(mps, vsb)]
    wus = [jnp.dot(tp, jnp.concatenate([kt_, stack_b(mv)], axis=1),
                   preferred_element_type=F32)
           for tp, kt_, mv in zip(tps, kts, mvs)]
    wss = [stack(wu[:, :n2]) for wu in wus]
    uss = [stack(wu[:, n2:]) for wu in wus]
    awus = [jnp.dot(arb, jnp.concatenate([ws, us], axis=1).astype(BF16), preferred_element_type=F32)
            for arb, ws, us in zip(arbs, wss, uss)]
    qs = [(rt_ - awu[:, :n2]).astype(BF16) for rt_, awu in zip(rtn, awus)]
    ylocs = [jnp.dot(ark, v_, preferred_element_type=F32) - awu[:, n2:]
             for ark, v_, awu in zip(arks, vsb, awus)]
    gms = [(-_dot(ws.T, bt_) * pc).astype(BF16) for ws, bt_, pc in zip(wss, bts, pcs)]
    hms = [_dot(jnp.concatenate([v_, -us], axis=0).T, kb) * pc
           for v_, us, kb, pc in zip(vs, uss, kbs, pcs)]

    states = [s_ref[p] for p in range(n_pairs)]
    for idx, (ci, p) in enumerate(items):
        sb = states[p].astype(BF16)
        y_ref[CHUNK * ci:CHUNK * (ci + 1), LANES * p:LANES * (p + 1)] = (
            _dot_nt(qs[idx], sb) + ylocs[idx])
        states[p] = (states[p] * pcs[idx] + jnp.dot(sb, gms[idx], preferred_element_type=F32)
                     + hms[idx])
    for p in range(n_pairs):
        s_ref[p] = states[p]

    y = y_ref[...]
    dlt = y - head_sum(y, 2) * (1.0 / RW_HEAD)
    var = head_sum(dlt * dlt, 2) * (1.0 / RW_HEAD)
    yn = dlt * lax.rsqrt(var + GN_EPS) * lng_ref[...] + lnb_ref[...]
    bonus = head_sum(r * km * rk_ref[...], 1) * v
    o_ref[...] = ((yn + bonus) * g_ref[...]).astype(o_ref.dtype)


def rw_rec(rkv, lw, a, g, k_k, k_a, r_k, ln_g, ln_b, batch, n_chunks=4, n_pairs=4):
    _, t, d = rkv.shape
    seq = t // batch
    tt = n_chunks * CHUNK
    lwid = n_pairs * LANES
    nc = seq // tt

    def rkv_spec(m):
        return pl.BlockSpec((None, tt, lwid), lambda b, l, c: (m, b * nc + c, l))

    act_spec = pl.BlockSpec((tt, lwid), lambda b, l, c: (b * nc + c, l))
    vec_spec = pl.BlockSpec((1, lwid), lambda b, l, c: (0, l))
    vecs = [x.reshape(1, d) for x in (k_k, k_a, r_k, ln_g, ln_b)]
    return pl.pallas_call(
        functools.partial(_rw_rec_kernel, n_chunks=n_chunks, n_pairs=n_pairs),
        out_shape=jax.ShapeDtypeStruct((t, d), BF16),
        grid=(batch, d // lwid, nc),
        in_specs=[rkv_spec(0), rkv_spec(1), rkv_spec(2), act_spec, act_spec, act_spec]
                 + [vec_spec] * 5,
        out_specs=act_spec,
        scratch_shapes=[pltpu.VMEM((n_pairs, LANES, LANES), F32), pltpu.VMEM((tt, lwid), F32)],
        compiler_params=_params(("parallel", "parallel", "arbitrary")),
        name="rw_rec",
    )(rkv, rkv, rkv, lw, a, g, *vecs)


def _pad_axis(x, axis, size):
    pad = [(0, 0)] * x.ndim
    pad[axis] = (0, size - x.shape[axis])
    return jnp.pad(x, pad)


def kernel(x, norm_mix_g, norm_mlp_g, w_mlp_up, w_mlp_down, final_norm_g, rg_w_in, rg_conv_w, rg_conv_b, rg_gx_w, rg_gx_b, rg_ga_w, rg_ga_b, rg_lambda, rg_w_out, rw_mu, rw_w_rkv, rw_w0, rw_w1, rw_w2, rw_a0, rw_a1, rw_a2, rw_g1, rw_g2, rw_k_k, rw_k_a, rw_r_k, rw_ln_g, rw_ln_b, rw_w_out):
    batch, seq, d = x.shape
    depth = norm_mix_g.shape[0]
    t = batch * seq
    h = x.reshape(t, d)
    w_down_b = w_mlp_down.astype(BF16)
    rg_w_out_b, rw_w_out_b = rg_w_out.astype(BF16), rw_w_out.astype(BF16)
    rg_gx_b16, rg_ga_b16 = rg_gx_w.astype(BF16), rg_ga_w.astype(BF16)
    rw_w_rkv_b = rw_w_rkv.astype(BF16)
    hn = rmsnorm(h, norm_mix_g[0], BF16)
    for i in range(depth):
        j = i // 2
        if i % 2 == 0:
            gate = mm_act(hn, rg_w_in, j, "gelu", F32, col0=0, n=d)
            rec = mm_act(hn, rg_w_in, j, "none", F32, col0=d, n=d)
            y = rg_scan(rec, gate, rg_conv_w[j], rg_conv_b[j], rg_gx_b16, rg_gx_b[j],
                        rg_ga_b16, rg_ga_b[j], rg_lambda[j], j, batch)
            w_out = rg_w_out_b
        else:
            rkv = rw_proj(hn, rw_mu[j, :3], rw_w_rkv_b, j, seq)
            lora = LANES
            lw, a, g = rw_lora(
                hn, rw_mu[j, 3:], rw_w0[j],
                _pad_axis(rw_w1[j], 1, lora).astype(BF16), _pad_axis(rw_w2[j], 0, lora).astype(BF16),
                rw_a0[j],
                _pad_axis(rw_a1[j], 1, lora).astype(BF16), _pad_axis(rw_a2[j], 0, lora).astype(BF16),
                rw_g1[j].astype(BF16), rw_g2[j].astype(BF16), seq)
            y = rw_rec(rkv, lw, a, g, rw_k_k[j], rw_k_a[j], rw_r_k[j].reshape(d),
                       rw_ln_g[j], rw_ln_b[j], batch)
            w_out = rw_w_out_b
        h, hn = mm_res_norm(y, w_out, j, h, norm_mlp_g[i], BF16, tk=d)
        u = mm_act(hn, w_mlp_up, i, "relu2", BF16)
        last = i + 1 == depth
        g_next = final_norm_g if last else norm_mix_g[i + 1]
        hn_dtype = F32 if (last or (i + 1) % 2 == 1) else BF16
        h, hn = mm_res_norm(u, w_down_b, i, h, g_next, hn_dtype, emit_h=not last)
    return hn.reshape(batch, seq, d)
```

```python
import functools

import jax
import jax.numpy as jnp
from jax import lax
from jax.experimental import pallas as pl
from jax.experimental.pallas import tpu as pltpu

F32 = jnp.float32
BF16 = jnp.bfloat16

RMS_EPS = 1e-6
GN_EPS = 64e-5
RG_C = 8.0
RG_HEADS = 8
CONV_W = 4
RW_HEAD = 64
LANES = 128
MXU_DIM = 256
SUBLANES = 8
CHUNK = 64
VMEM_LIMIT = 56 * 1024 * 1024


def _params(sem):
    return pltpu.CompilerParams(dimension_semantics=sem, vmem_limit_bytes=VMEM_LIMIT)


def _dot(a, b):
    return jnp.dot(a.astype(BF16), b.astype(BF16), preferred_element_type=F32)


def _dot_nt(a, b):
    return lax.dot_general(a.astype(BF16), b.astype(BF16), (((1,), (1,)), ((), ())),
                           preferred_element_type=F32)


def _gelu_tanh(x):
    return 0.5 * x * (1.0 + jnp.tanh(0.7978845608028654 * (x + 0.044715 * (x * x * x))))


def _softplus(x):
    return jnp.maximum(x, 0.0) + jnp.log1p(jnp.exp(-jnp.abs(x)))


def _rmsnorm_kernel(x_ref, g_ref, o_ref):
    x = x_ref[...]
    ms = jnp.mean(x * x, axis=-1, keepdims=True)
    o_ref[...] = (x * lax.rsqrt(ms + RMS_EPS) * g_ref[...]).astype(o_ref.dtype)


def rmsnorm(x, g, out_dtype, tm=512):
    t, d = x.shape
    return pl.pallas_call(
        _rmsnorm_kernel,
        out_shape=jax.ShapeDtypeStruct((t, d), out_dtype),
        grid=(t // tm,),
        in_specs=[pl.BlockSpec((tm, d), lambda i: (i, 0)),
                  pl.BlockSpec((1, d), lambda i: (0, 0))],
        out_specs=pl.BlockSpec((tm, d), lambda i: (i, 0)),
        compiler_params=_params(("parallel",)),
        name="rmsnorm",
    )(x, g.reshape(1, d))


def _mm_act_kernel(a_ref, w_ref, *refs, act, with_cast):
    if with_cast:
        c_ref, o_ref, cb_ref, wb_ref = refs
        cb_ref[...] = c_ref[...].astype(BF16)
    else:
        o_ref, wb_ref = refs

    @pl.when(pl.program_id(1) == 0)
    def _():
        wb_ref[...] = w_ref[...].astype(BF16)

    y = jnp.dot(a_ref[...], wb_ref[...], preferred_element_type=F32)
    if act == "relu2":
        y = jnp.square(jnp.maximum(y, 0.0))
    elif act == "gelu":
        y = _gelu_tanh(y)
    o_ref[...] = y.astype(o_ref.dtype)


def mm_act(a, w, layer, act, out_dtype, col0=0, n=None, cast=None, tm=1024, tn=1024):
    t, k = a.shape
    n = w.shape[2] if n is None else n
    j0 = col0 // tn
    nj, ni = n // tn, t // tm
    in_specs = [pl.BlockSpec((tm, k), lambda j, i: (i, 0)),
                pl.BlockSpec((None, k, tn), lambda j, i: (layer, 0, j0 + j))]
    out_shape = [jax.ShapeDtypeStruct((t, n), out_dtype)]
    out_specs = [pl.BlockSpec((tm, tn), lambda j, i: (i, j))]
    args = [a, w]
    if cast is not None:
        _, rows, cols = cast.shape
        slab = rows // (nj * ni)
        in_specs.append(pl.BlockSpec((None, slab, cols), lambda j, i: (layer, j * ni + i, 0)))
        out_shape.append(jax.ShapeDtypeStruct((rows, cols), BF16))
        out_specs.append(pl.BlockSpec((slab, cols), lambda j, i: (j * ni + i, 0)))
        args.append(cast)
    res = pl.pallas_call(
        functools.partial(_mm_act_kernel, act=act, with_cast=cast is not None),
        out_shape=out_shape,
        grid=(nj, ni),
        in_specs=in_specs,
        out_specs=out_specs,
        scratch_shapes=[pltpu.VMEM((k, tn), BF16)],
        compiler_params=_params(("parallel", "arbitrary")),
        name="mm_" + act,
    )(*args)
    return res if cast is not None else res[0]


def _mm_res_norm_kernel(a_ref, w_ref, h_ref, g_ref, *refs, emit_h, nk):
    refs = list(refs)
    hnew_ref = refs.pop(0) if emit_h else None
    hn_ref = refs.pop(0)
    kk = pl.program_id(1)

    def part():
        return jnp.dot(a_ref[...], w_ref[...], preferred_element_type=F32)

    def norm(hnew):
        ms = jnp.mean(hnew * hnew, axis=-1, keepdims=True)
        hn_ref[...] = (hnew * lax.rsqrt(ms + RMS_EPS) * g_ref[...]).astype(hn_ref.dtype)

    if nk == 1:
        hnew = h_ref[...] + part()
        if emit_h:
            hnew_ref[...] = hnew
        norm(hnew)
        return
    acc_ref = hnew_ref if emit_h else refs.pop(0)

    @pl.when(kk == 0)
    def _():
        acc_ref[...] = h_ref[...] + part()

    @pl.when(kk > 0)
    def _():
        acc_ref[...] += part()

    @pl.when(kk == nk - 1)
    def _():
        norm(acc_ref[...])


def mm_res_norm(a, w, layer, h, g, hn_dtype, emit_h=True, tm=512, tk=1024):
    t, k = a.shape
    d = w.shape[-1]
    nk = k // tk
    if layer is None:
        w_spec = pl.BlockSpec((tk, d), lambda i, kk: (kk, 0))
    else:
        w_spec = pl.BlockSpec((None, tk, d), lambda i, kk: (layer, kk, 0))
    need_scratch = nk > 1 and not emit_h
    row_spec = pl.BlockSpec((tm, d), lambda i, kk: (i, 0))
    out_shape = [jax.ShapeDtypeStruct((t, d), hn_dtype)]
    out_specs = [row_spec]
    if emit_h:
        out_shape = [jax.ShapeDtypeStruct((t, d), F32)] + out_shape
        out_specs = [row_spec] + out_specs
    res = pl.pallas_call(
        functools.partial(_mm_res_norm_kernel, emit_h=emit_h, nk=nk),
        out_shape=out_shape,
        grid=(t // tm, nk),
        in_specs=[pl.BlockSpec((tm, tk), lambda i, kk: (i, kk)),
                  w_spec,
                  row_spec,
                  pl.BlockSpec((1, d), lambda i, kk: (0, 0))],
        out_specs=out_specs,
        scratch_shapes=[pltpu.VMEM((tm, d), F32)] if need_scratch else [],
        compiler_params=_params(("parallel", "arbitrary")),
        name="mm_res_norm",
    )(a, w, h, g.reshape(1, d))
    if emit_h:
        return res[0], res[1]
    return None, res[0]


def _rg_scan_kernel(rec_a, rec_b, gate_a, gate_b, cw_ref, cb_ref, gxw_ref, gxb_ref, gaw_ref,
                    gab_ref, lam_ref, y_ref, tail_ref, hc_ref, ys_a, ys_b, *, tt):
    c = pl.program_id(2)
    nph = SUBLANES
    ng = tt // nph

    @pl.when(c == 0)
    def _():
        tail_ref[...] = jnp.zeros_like(tail_ref)
        hc_ref[...] = jnp.zeros_like(hc_ref)

    def phase(ref_a, ref_b, p):
        rows = pl.ds(p, ng, stride=nph)
        return jnp.concatenate([ref_a[rows, :], ref_b[rows, :]], axis=1)

    grow = lax.broadcasted_iota(jnp.int32, (ng, 2 * LANES), 0)

    def prev_group(arr, first):
        return jnp.where(grow == 0, first, pltpu.roll(arr, 1, 0))

    xs = [phase(rec_a, rec_b, p) for p in range(nph)]
    tail = tail_ref[...]
    xprev = {p: prev_group(xs[p], tail[p:p + 1]) for p in range(nph - CONV_W + 1, nph)}
    tail_ref[...] = jnp.concatenate([rec_a[tt - nph:tt, :], rec_b[tt - nph:tt, :]], axis=1)
    xcs = []
    for p in range(nph):
        acc = cb_ref[...] + cw_ref[CONV_W - 1:CONV_W, :] * xs[p]
        for j in range(1, CONV_W):
            src = xs[p - j] if p >= j else xprev[p - j + nph]
            acc = acc + cw_ref[CONV_W - 1 - j:CONV_W - j, :] * src
        xcs.append(acc)
    xc = jnp.concatenate(xcs, axis=0)

    xb = xc.astype(BF16)
    i_t = jax.nn.sigmoid(jnp.dot(xb, gxw_ref[...], preferred_element_type=F32) + gxb_ref[...])
    r_t = jax.nn.sigmoid(jnp.dot(xb, gaw_ref[...], preferred_element_type=F32) + gab_ref[...])
    log_sig_lam = -_softplus(-lam_ref[...])
    log_a = RG_C * r_t * log_sig_lam
    a = jnp.exp(log_a)
    th = jnp.tanh(log_a)
    mult = jnp.sqrt(jnp.maximum(-2.0 * th / (1.0 - th), 0.0))
    b = mult * (i_t * xc)

    acs, bcs = [a[0:ng]], [b[0:ng]]
    for p in range(1, nph):
        ap, bp = a[ng * p:ng * (p + 1)], b[ng * p:ng * (p + 1)]
        acs.append(ap * acs[-1])
        bcs.append(ap * bcs[-1] + bp)
    pa, pb = acs[-1], bcs[-1]
    s = 1
    while s < ng:
        valid = grow >= s
        pb = jnp.where(valid, pa * pltpu.roll(pb, s, 0) + pb, pb)
        pa = jnp.where(valid, pa * pltpu.roll(pa, s, 0), pa)
        s *= 2
    hc = hc_ref[...]
    hend = pa * hc + pb
    hin = prev_group(hend, hc)
    hc_ref[...] = hend[ng - 1:ng]
    for p in range(nph):
        yp = (acs[p] * hin + bcs[p]) * phase(gate_a, gate_b, p)
        ys_a[pl.ds(p, ng, stride=nph), :] = yp[:, :LANES]
        ys_b[pl.ds(p, ng, stride=nph), :] = yp[:, LANES:]
    y_ref[...] = jnp.concatenate([ys_a[...], ys_b[...]], axis=1).astype(y_ref.dtype)


def rg_scan(rec, gate, conv_w, conv_b, gx_w, gx_b, ga_w, ga_b, lam, layer, batch, tt=512):
    t, d = rec.shape
    seq = t // batch
    nc = seq // tt
    blk = d // RG_HEADS
    assert blk == 2 * LANES
    act_spec = pl.BlockSpec((tt, blk), lambda b, h, c: (b * nc + c, h))
    half_a = pl.BlockSpec((tt, LANES), lambda b, h, c: (b * nc + c, 2 * h))
    half_b = pl.BlockSpec((tt, LANES), lambda b, h, c: (b * nc + c, 2 * h + 1))
    vec_spec = pl.BlockSpec((1, blk), lambda b, h, c: (0, h))
    w_spec = pl.BlockSpec((None, None, blk, blk), lambda b, h, c: (layer, h, 0, 0))
    return pl.pallas_call(
        functools.partial(_rg_scan_kernel, tt=tt),
        out_shape=jax.ShapeDtypeStruct((t, d), BF16),
        grid=(batch, RG_HEADS, nc),
        in_specs=[half_a, half_b, half_a, half_b,
                  pl.BlockSpec((CONV_W, blk), lambda b, h, c: (0, h)), vec_spec,
                  w_spec, vec_spec, w_spec, vec_spec, vec_spec],
        out_specs=act_spec,
        scratch_shapes=[pltpu.VMEM((SUBLANES, blk), F32), pltpu.VMEM((1, blk), F32),
                        pltpu.VMEM((tt, LANES), F32), pltpu.VMEM((tt, LANES), F32)],
        compiler_params=_params(("parallel", "parallel", "arbitrary")),
        name="rg_scan",
    )(rec, rec, gate, gate, conv_w, conv_b.reshape(1, d), gx_w, gx_b.reshape(1, d),
      ga_w, ga_b.reshape(1, d), lam.reshape(1, d))


PREV_ROWS = 16


def _prev_spec(tm, d, n_lead):
    def index(*ids):
        return (jnp.maximum(ids[n_lead] * (tm // PREV_ROWS) - 1, 0), 0)
    return pl.BlockSpec((PREV_ROWS, d), index)


def _token_shift_delta(x, prev_ref, is_first):
    prev_last = prev_ref[...].astype(F32)[PREV_ROWS - 1:PREV_ROWS, :]
    prev_last = jnp.where(is_first, 0.0, prev_last)
    row = lax.broadcasted_iota(jnp.int32, x.shape, 0)
    xs = jnp.where(row == 0, prev_last, pltpu.roll(x, 1, 0))
    return xs - x


def _rw_proj_kernel(x_ref, prev_ref, mu_ref, w_ref, o_ref, *, blocks_per_seq):
    i = pl.program_id(1)
    x = x_ref[...].astype(F32)
    xx = _token_shift_delta(x, prev_ref, (i % blocks_per_seq) == 0)
    xm = (x + xx * mu_ref[...]).astype(BF16)
    o_ref[...] = jnp.dot(xm, w_ref[...], preferred_element_type=F32)


def rw_proj(hn, mu, w_rkv, layer, seq, tm=512):
    t, d = hn.shape
    nmat = w_rkv.shape[1]
    return pl.pallas_call(
        functools.partial(_rw_proj_kernel, blocks_per_seq=seq // tm),
        out_shape=jax.ShapeDtypeStruct((nmat, t, d), F32),
        grid=(nmat, t // tm),
        in_specs=[pl.BlockSpec((tm, d), lambda j, i: (i, 0)),
                  _prev_spec(tm, d, 1),
                  pl.BlockSpec((None, 1, d), lambda j, i: (j, 0, 0)),
                  pl.BlockSpec((None, None, d, d), lambda j, i: (layer, j, 0, 0))],
        out_specs=pl.BlockSpec((None, tm, d), lambda j, i: (j, i, 0)),
        compiler_params=_params(("parallel", "parallel")),
        name="rw_proj",
    )(hn, hn, mu.reshape(nmat, 1, d), w_rkv)


def _rw_lora_kernel(x_ref, prev_ref, mu_ref, w0_ref, w1_ref, w2_ref, a0_ref, a1_ref, a2_ref,
                    g1_ref, g2_ref, lw_ref, a_ref, g_ref, *, blocks_per_seq):
    i = pl.program_id(0)
    x = x_ref[...].astype(F32)
    xx = _token_shift_delta(x, prev_ref, (i % blocks_per_seq) == 0)
    xw = (x + xx * mu_ref[0:1, :]).astype(BF16)
    xa = (x + xx * mu_ref[1:2, :]).astype(BF16)
    xg = (x + xx * mu_ref[2:3, :]).astype(BF16)
    hid = jnp.tanh(jnp.dot(xw, w1_ref[...], preferred_element_type=F32))
    w = w0_ref[...] + _dot(hid, w2_ref[...])
    lw_ref[...] = -0.6065306597126334 * jax.nn.sigmoid(w)
    hid = jnp.dot(xa, a1_ref[...], preferred_element_type=F32)
    a_ref[...] = jax.nn.sigmoid(a0_ref[...] + _dot(hid, a2_ref[...]))
    hid = jax.nn.sigmoid(jnp.dot(xg, g1_ref[...], preferred_element_type=F32))
    g_ref[...] = _dot(hid, g2_ref[...])


def rw_lora(hn, mu3, w0, w1, w2, a0, a1, a2, g1, g2, seq, tm=512):
    t, d = hn.shape
    row_spec = pl.BlockSpec((tm, d), lambda i: (i, 0))

    def full(arr):
        return pl.BlockSpec(arr.shape, lambda i: (0,) * arr.ndim)

    w0, a0 = w0.reshape(1, d), a0.reshape(1, d)
    consts = (mu3, w0, w1, w2, a0, a1, a2, g1, g2)
    return pl.pallas_call(
        functools.partial(_rw_lora_kernel, blocks_per_seq=seq // tm),
        out_shape=[jax.ShapeDtypeStruct((t, d), F32)] * 3,
        grid=(t // tm,),
        in_specs=[row_spec, _prev_spec(tm, d, 0)] + [full(c) for c in consts],
        out_specs=[row_spec] * 3,
        compiler_params=_params(("parallel",)),
        name="rw_lora",
    )(hn, hn, *consts)


def _split_terms(x, passes):
    terms = []
    for i in range(passes):
        part = x.astype(BF16)
        terms.append(part)
        if i + 1 < passes:
            x = x - part.astype(F32)
    return terms


def _unit_lower_inverse_many(lps, row, col, eye, mul):
    diag8 = jnp.right_shift(row, 3) == jnp.right_shift(col, 3)
    lds = [jnp.where(diag8, lp, 0.0) for lp in lps]
    l2s = [mul(ld, ld) for ld in lds]
    l4s = [mul(l2, l2) for l2 in l2s]
    ts = [mul(eye - ld, eye + l2) for ld, l2 in zip(lds, l2s)]
    ts = [mul(t, eye + l4) for t, l4 in zip(ts, l4s)]
    for sh in (3, 4, 5):
        rb, cb = jnp.right_shift(row, sh), jnp.right_shift(col, sh)
        sel = (jnp.right_shift(rb, 1) == jnp.right_shift(cb, 1)) & (rb != cb)
        ots = [mul(jnp.where(sel, lp, 0.0), t) for lp, t in zip(lps, ts)]
        ts = [t - mul(t, ot) for t, ot in zip(ts, ots)]
    return ts


def _rw_rec_kernel(r_ref, k_ref, v_ref, lw_ref, a_ref, g_ref, kk_ref, ka_ref, rk_ref,
                   lng_ref, lnb_ref, o_ref, s_ref, y_ref, *, n_chunks, n_pairs):
    c = pl.program_id(2)

    @pl.when(c == 0)
    def _():
        s_ref[...] = jnp.zeros_like(s_ref)

    tt, lwid = n_chunks * CHUNK, n_pairs * LANES
    n2 = 2 * CHUNK
    shift = CHUNK.bit_length() - 1
    row = lax.broadcasted_iota(jnp.int32, (CHUNK, n2), 0)
    col = jnp.bitwise_and(lax.broadcasted_iota(jnp.int32, (CHUNK, n2), 1), CHUNK - 1)
    strict = row > col
    incl = row >= col
    eye = (row == col).astype(F32)
    trow = lax.broadcasted_iota(jnp.int32, (tt, tt), 0)
    tcol = lax.broadcasted_iota(jnp.int32, (tt, tt), 1)
    tri = ((jnp.right_shift(trow, shift) == jnp.right_shift(tcol, shift))
           & (trow >= tcol)).astype(BF16)
    hw = min(lwid, MXU_DIM)
    hrow = lax.broadcasted_iota(jnp.int32, (hw, hw), 0)
    hcol = lax.broadcasted_iota(jnp.int32, (hw, hw), 1)
    head_ones = (jnp.right_shift(hrow, shift) == jnp.right_shift(hcol, shift)).astype(BF16)
    lane = lax.broadcasted_iota(jnp.int32, (CHUNK, n2), 1)
    m0 = (lane < RW_HEAD).astype(F32)
    m1 = 1.0 - m0
    m0b, m1b = m0.astype(BF16), m1.astype(BF16)

    def head_sum(x, passes):
        cols = []
        for c0 in range(0, lwid, hw):
            cols.append(sum(jnp.dot(p, head_ones, preferred_element_type=F32)
                            for p in _split_terms(x[:, c0:c0 + hw], passes)))
        return jnp.concatenate(cols, axis=1)

    def stack(x):
        return jnp.concatenate([x * m0, x * m1], axis=0)

    def stack_b(x):
        xb = x.astype(BF16)
        return jnp.concatenate([xb * m0b, xb * m1b], axis=0)

    r, k, v = r_ref[...], k_ref[...], v_ref[...]
    lw, a = lw_ref[...], a_ref[...]
    kkp = k * kk_ref[...]
    kap = kkp * lax.rsqrt(jnp.maximum(head_sum(kkp * kkp, 1), 1e-24))
    b = kap * a
    km = k * (1.0 + (a - 1.0) * ka_ref[...])
    cum = sum(jnp.dot(tri, p, preferred_element_type=F32) for p in _split_terms(lw, 3))
    e_in = jnp.exp(cum)
    e_neg = jnp.exp(-cum)
    kt = kap * jnp.exp(cum - lw)
    bt = b * e_neg
    kmt = km * e_neg
    rt = r * e_in

    items = [(ci, p) for ci in range(n_chunks) for p in range(n_pairs)]

    def blk(x, ci, p):
        return x[CHUNK * ci:CHUNK * (ci + 1), LANES * p:LANES * (p + 1)]

    def mul(a_sb, b_sb):
        return jnp.dot(a_sb.astype(BF16), stack_b(b_sb), preferred_element_type=F32)

    ktn = [blk(kt, *it) for it in items]
    rtn = [blk(rt, *it) for it in items]
    kts = [stack_b(x) for x in ktn]
    bts = [stack_b(blk(bt, *it)) for it in items]
    kms = [stack_b(blk(kmt, *it)) for it in items]
    vs = [stack(blk(v, *it)) for it in items]
    vsb = [x.astype(BF16) for x in vs]
    pcs = [blk(e_in, *it)[CHUNK - 1:CHUNK, :] for it in items]
    kbs = [jnp.concatenate([km_, bt_], axis=0) for km_, bt_ in zip(kms, bts)]
    amats = [_dot_nt(jnp.concatenate([kt_, rt_], axis=0), kb)
             for kt_, rt_, kb in zip(ktn, rtn, kbs)]
    lps = [jnp.where(strict, am[:CHUNK, n2:], 0.0) for am in amats]
    mps = [jnp.where(strict, am[:CHUNK, :n2], 0.0).astype(BF16) for am in amats]
    arbs = [jnp.where(incl, am[CHUNK:, n2:], 0.0).astype(BF16) for am in amats]
    arks = [jnp.where(incl, am[CHUNK:, :n2], 0.0).astype(BF16) for am in amats]
    tps = [t.astype(BF16) for t in _unit_lower_inverse_many(lps, row, col, eye, mul)]
    mvs = [jnp.dot(mp, v_, preferred_element_type=F32) for mp, v_ in zip(mps, vsb)]
    wus = [jnp.dot(tp, jnp.concatenate([kt_, stack_b(mv)], axis=1),
                   preferred_element_type=F32)
           for tp, kt_, mv in zip(tps, kts, mvs)]
    wss = [stack(wu[:, :n2]) for wu in wus]
    uss = [stack(wu[:, n2:]) for wu in wus]
    awus = [jnp.dot(arb, jnp.concatenate([ws, us], axis=1).astype(BF16), preferred_element_type=F32)
            for arb, ws, us in zip(arbs, wss, uss)]
    qs = [(rt_ - awu[:, :n2]).astype(BF16) for rt_, awu in zip(rtn, awus)]
    ylocs = [jnp.dot(ark, v_, preferred_element_type=F32) - awu[:, n2:]
             for ark, v_, awu in zip(arks, vsb, awus)]
    gms = [(-_dot(ws.T, bt_) * pc).astype(BF16) for ws, bt_, pc in zip(wss, bts, pcs)]
    hms = [_dot(jnp.concatenate([v_, -us], axis=0).T, kb) * pc
           for v_, us, kb, pc in zip(vs, uss, kbs, pcs)]

    states = [s_ref[p] for p in range(n_pairs)]
    for idx, (ci, p) in enumerate(items):
        sb = states[p].astype(BF16)
        y_ref[CHUNK * ci:CHUNK * (ci + 1), LANES * p:LANES * (p + 1)] = (
            _dot_nt(qs[idx], sb) + ylocs[idx])
        states[p] = (states[p] * pcs[idx] + jnp.dot(sb, gms[idx], preferred_element_type=F32)
                     + hms[idx])
    for p in range(n_pairs):
        s_ref[p] = states[p]

    y = y_ref[...]
    dlt = y - head_sum(y, 2) * (1.0 / RW_HEAD)
    var = head_sum(dlt * dlt, 2) * (1.0 / RW_HEAD)
    yn = dlt * lax.rsqrt(var + GN_EPS) * lng_ref[...] + lnb_ref[...]
    bonus = head_sum(r * km * rk_ref[...], 1) * v
    o_ref[...] = ((yn + bonus) * g_ref[...]).astype(o_ref.dtype)


def rw_rec(rkv, lw, a, g, k_k, k_a, r_k, ln_g, ln_b, batch, n_chunks=4, n_pairs=4):
    _, t, d = rkv.shape
    seq = t // batch
    tt = n_chunks * CHUNK
    lwid = n_pairs * LANES
    nc = seq // tt

    def rkv_spec(m):
        return pl.BlockSpec((None, tt, lwid), lambda b, l, c: (m, b * nc + c, l))

    act_spec = pl.BlockSpec((tt, lwid), lambda b, l, c: (b * nc + c, l))
    vec_spec = pl.BlockSpec((1, lwid), lambda b, l, c: (0, l))
    vecs = [x.reshape(1, d) for x in (k_k, k_a, r_k, ln_g, ln_b)]
    return pl.pallas_call(
        functools.partial(_rw_rec_kernel, n_chunks=n_chunks, n_pairs=n_pairs),
        out_shape=jax.ShapeDtypeStruct((t, d), BF16),
        grid=(batch, d // lwid, nc),
        in_specs=[rkv_spec(0), rkv_spec(1), rkv_spec(2), act_spec, act_spec, act_spec]
                 + [vec_spec] * 5,
        out_specs=act_spec,
        scratch_shapes=[pltpu.VMEM((n_pairs, LANES, LANES), F32), pltpu.VMEM((tt, lwid), F32)],
        compiler_params=_params(("parallel", "parallel", "arbitrary")),
        name="rw_rec",
    )(rkv, rkv, rkv, lw, a, g, *vecs)


def _pad_axis(x, axis, size):
    pad = [(0, 0)] * x.ndim
    pad[axis] = (0, size - x.shape[axis])
    return jnp.pad(x, pad)


def kernel(x, norm_mix_g, norm_mlp_g, w_mlp_up, w_mlp_down, final_norm_g, rg_w_in, rg_conv_w, rg_conv_b, rg_gx_w, rg_gx_b, rg_ga_w, rg_ga_b, rg_lambda, rg_w_out, rw_mu, rw_w_rkv, rw_w0, rw_w1, rw_w2, rw_a0, rw_a1, rw_a2, rw_g1, rw_g2, rw_k_k, rw_k_a, rw_r_k, rw_ln_g, rw_ln_b, rw_w_out):
    batch, seq, d = x.shape
    depth = norm_mix_g.shape[0]
    t = batch * seq
    h = x.reshape(t, d)
    rg_w_out_b, rw_w_out_b = rg_w_out.astype(BF16), rw_w_out.astype(BF16)
    rg_gx_b16, rg_ga_b16 = rg_gx_w.astype(BF16), rg_ga_w.astype(BF16)
    rw_w_rkv_b = rw_w_rkv.astype(BF16)
    hn = rmsnorm(h, norm_mix_g[0], BF16)
    for i in range(depth):
        j = i // 2
        if i % 2 == 0:
            gate = mm_act(hn, rg_w_in, j, "gelu", F32, col0=0, n=d)
            rec = mm_act(hn, rg_w_in, j, "none", F32, col0=d, n=d)
            y = rg_scan(rec, gate, rg_conv_w[j], rg_conv_b[j], rg_gx_b16, rg_gx_b[j],
                        rg_ga_b16, rg_ga_b[j], rg_lambda[j], j, batch)
            w_out = rg_w_out_b
        else:
            rkv = rw_proj(hn, rw_mu[j, :3], rw_w_rkv_b, j, seq)
            lora = LANES
            lw, a, g = rw_lora(
                hn, rw_mu[j, 3:], rw_w0[j],
                _pad_axis(rw_w1[j], 1, lora).astype(BF16), _pad_axis(rw_w2[j], 0, lora).astype(BF16),
                rw_a0[j],
                _pad_axis(rw_a1[j], 1, lora).astype(BF16), _pad_axis(rw_a2[j], 0, lora).astype(BF16),
                rw_g1[j].astype(BF16), rw_g2[j].astype(BF16), seq)
            y = rw_rec(rkv, lw, a, g, rw_k_k[j], rw_k_a[j], rw_r_k[j].reshape(d),
                       rw_ln_g[j], rw_ln_b[j], batch)
            w_out = rw_w_out_b
        h, hn = mm_res_norm(y, w_out, j, h, norm_mlp_g[i], BF16, tk=d)
        u, w_down_b = mm_act(hn, w_mlp_up, i, "relu2", BF16, cast=w_mlp_down)
        last = i + 1 == depth
        g_next = final_norm_g if last else norm_mix_g[i + 1]
        h, hn = mm_res_norm(u, w_down_b, None, h, g_next, F32 if last else BF16, emit_h=not last,
                            tm=1024, tk=512)
    return hn.reshape(batch, seq, d)
```

```python
import functools

import jax
import jax.numpy as jnp
from jax import lax
from jax.experimental import pallas as pl
from jax.experimental.pallas import tpu as pltpu

F32 = jnp.float32
BF16 = jnp.bfloat16

RMS_EPS = 1e-6
GN_EPS = 64e-5
RG_C = 8.0
RG_HEADS = 8
CONV_W = 4
RW_HEAD = 64
LANES = 128
MXU_DIM = 256
SUBLANES = 8
CHUNK = 64
VMEM_LIMIT = 56 * 1024 * 1024


def _params(sem):
    return pltpu.CompilerParams(dimension_semantics=sem, vmem_limit_bytes=VMEM_LIMIT)


def _dot(a, b):
    return jnp.dot(a.astype(BF16), b.astype(BF16), preferred_element_type=F32)


def _dot_nt(a, b):
    return lax.dot_general(a.astype(BF16), b.astype(BF16), (((1,), (1,)), ((), ())),
                           preferred_element_type=F32)


def _gelu_tanh(x):
    return 0.5 * x * (1.0 + jnp.tanh(0.7978845608028654 * (x + 0.044715 * (x * x * x))))


def _softplus(x):
    return jnp.maximum(x, 0.0) + jnp.log1p(jnp.exp(-jnp.abs(x)))


def _rmsnorm_kernel(x_ref, g_ref, o_ref):
    x = x_ref[...]
    ms = jnp.mean(x * x, axis=-1, keepdims=True)
    o_ref[...] = (x * lax.rsqrt(ms + RMS_EPS) * g_ref[...]).astype(o_ref.dtype)


def rmsnorm(x, g, out_dtype, tm=512):
    t, d = x.shape
    return pl.pallas_call(
        _rmsnorm_kernel,
        out_shape=jax.ShapeDtypeStruct((t, d), out_dtype),
        grid=(t // tm,),
        in_specs=[pl.BlockSpec((tm, d), lambda i: (i, 0)),
                  pl.BlockSpec((1, d), lambda i: (0, 0))],
        out_specs=pl.BlockSpec((tm, d), lambda i: (i, 0)),
        compiler_params=_params(("parallel",)),
        name="rmsnorm",
    )(x, g.reshape(1, d))


def _mm_act_kernel(a_ref, w_ref, *refs, act, with_cast):
    if with_cast:
        c_ref, o_ref, cb_ref, wb_ref = refs
        cb_ref[...] = c_ref[...].astype(BF16)
    else:
        o_ref, wb_ref = refs

    @pl.when(pl.program_id(1) == 0)
    def _():
        wb_ref[...] = w_ref[...].astype(BF16)

    y = jnp.dot(a_ref[...], wb_ref[...], preferred_element_type=F32)
    if act == "relu2":
        y = jnp.square(jnp.maximum(y, 0.0))
    elif act == "gelu":
        y = _gelu_tanh(y)
    o_ref[...] = y.astype(o_ref.dtype)


def mm_act(a, w, layer, act, out_dtype, col0=0, n=None, cast=None, tm=1024, tn=1024):
    t, k = a.shape
    n = w.shape[2] if n is None else n
    j0 = col0 // tn
    nj, ni = n // tn, t // tm
    in_specs = [pl.BlockSpec((tm, k), lambda j, i: (i, 0)),
                pl.BlockSpec((None, k, tn), lambda j, i: (layer, 0, j0 + j))]
    out_shape = [jax.ShapeDtypeStruct((t, n), out_dtype)]
    out_specs = [pl.BlockSpec((tm, tn), lambda j, i: (i, j))]
    args = [a, w]
    if cast is not None:
        _, rows, cols = cast.shape
        slab = rows // (nj * ni)
        in_specs.append(pl.BlockSpec((None, slab, cols), lambda j, i: (layer, j * ni + i, 0)))
        out_shape.append(jax.ShapeDtypeStruct((rows, cols), BF16))
        out_specs.append(pl.BlockSpec((slab, cols), lambda j, i: (j * ni + i, 0)))
        args.append(cast)
    res = pl.pallas_call(
        functools.partial(_mm_act_kernel, act=act, with_cast=cast is not None),
        out_shape=out_shape,
        grid=(nj, ni),
        in_specs=in_specs,
        out_specs=out_specs,
        scratch_shapes=[pltpu.VMEM((k, tn), BF16)],
        compiler_params=_params(("parallel", "arbitrary")),
        name="mm_" + act,
    )(*args)
    return res if cast is not None else res[0]


def _mm_res_norm_kernel(a_ref, w_ref, h_ref, g_ref, *refs, emit_h, nk):
    refs = list(refs)
    hnew_ref = refs.pop(0) if emit_h else None
    hn_ref = refs.pop(0)
    kk = pl.program_id(1)

    def part():
        return jnp.dot(a_ref[...], w_ref[...], preferred_element_type=F32)

    def norm(hnew, rows=slice(None)):
        ms = jnp.mean(hnew * hnew, axis=-1, keepdims=True)
        hn_ref[rows, :] = (hnew * lax.rsqrt(ms + RMS_EPS) * g_ref[...]).astype(hn_ref.dtype)

    if nk == 1:
        half = a_ref.shape[0] // 2
        for rows in (slice(0, half), slice(half, 2 * half)):
            hnew = h_ref[rows, :] + jnp.dot(a_ref[rows, :], w_ref[...],
                                            preferred_element_type=F32)
            if emit_h:
                hnew_ref[rows, :] = hnew
            norm(hnew, rows)
        return
    acc_ref = hnew_ref if emit_h else refs.pop(0)

    @pl.when(kk == 0)
    def _():
        acc_ref[...] = h_ref[...] + part()

    @pl.when(kk > 0)
    def _():
        acc_ref[...] += part()

    @pl.when(kk == nk - 1)
    def _():
        norm(acc_ref[...])


def mm_res_norm(a, w, layer, h, g, hn_dtype, emit_h=True, tm=512, tk=1024):
    t, k = a.shape
    d = w.shape[-1]
    nk = k // tk
    if layer is None:
        w_spec = pl.BlockSpec((tk, d), lambda i, kk: (kk, 0))
    else:
        w_spec = pl.BlockSpec((None, tk, d), lambda i, kk: (layer, kk, 0))
    need_scratch = nk > 1 and not emit_h
    row_spec = pl.BlockSpec((tm, d), lambda i, kk: (i, 0))
    out_shape = [jax.ShapeDtypeStruct((t, d), hn_dtype)]
    out_specs = [row_spec]
    if emit_h:
        out_shape = [jax.ShapeDtypeStruct((t, d), F32)] + out_shape
        out_specs = [row_spec] + out_specs
    res = pl.pallas_call(
        functools.partial(_mm_res_norm_kernel, emit_h=emit_h, nk=nk),
        out_shape=out_shape,
        grid=(t // tm, nk),
        in_specs=[pl.BlockSpec((tm, tk), lambda i, kk: (i, kk)),
                  w_spec,
                  row_spec,
                  pl.BlockSpec((1, d), lambda i, kk: (0, 0))],
        out_specs=out_specs,
        scratch_shapes=[pltpu.VMEM((tm, d), F32)] if need_scratch else [],
        compiler_params=_params(("parallel", "arbitrary")),
        name="mm_res_norm",
    )(a, w, h, g.reshape(1, d))
    if emit_h:
        return res[0], res[1]
    return None, res[0]


def _rg_scan_kernel(rec_a, rec_b, gate_a, gate_b, cw_ref, cb_ref, gxw_ref, gxb_ref, gaw_ref,
                    gab_ref, lam_ref, y_ref, tail_ref, hc_ref, ys_a, ys_b, *, tt):
    c = pl.program_id(2)
    nph = SUBLANES
    ng = tt // nph

    @pl.when(c == 0)
    def _():
        tail_ref[...] = jnp.zeros_like(tail_ref)
        hc_ref[...] = jnp.zeros_like(hc_ref)

    def phase(ref_a, ref_b, p):
        rows = pl.ds(p, ng, stride=nph)
        return jnp.concatenate([ref_a[rows, :], ref_b[rows, :]], axis=1)

    grow = lax.broadcasted_iota(jnp.int32, (ng, 2 * LANES), 0)

    def prev_group(arr, first):
        return jnp.where(grow == 0, first, pltpu.roll(arr, 1, 0))

    xs = [phase(rec_a, rec_b, p) for p in range(nph)]
    tail = tail_ref[...]
    xprev = {p: prev_group(xs[p], tail[p:p + 1]) for p in range(nph - CONV_W + 1, nph)}
    tail_ref[...] = jnp.concatenate([rec_a[tt - nph:tt, :], rec_b[tt - nph:tt, :]], axis=1)
    xcs = []
    for p in range(nph):
        acc = cb_ref[...] + cw_ref[CONV_W - 1:CONV_W, :] * xs[p]
        for j in range(1, CONV_W):
            src = xs[p - j] if p >= j else xprev[p - j + nph]
            acc = acc + cw_ref[CONV_W - 1 - j:CONV_W - j, :] * src
        xcs.append(acc)
    xc = jnp.concatenate(xcs, axis=0)

    xb = xc.astype(BF16)
    i_t = jax.nn.sigmoid(jnp.dot(xb, gxw_ref[...], preferred_element_type=F32) + gxb_ref[...])
    r_t = jax.nn.sigmoid(jnp.dot(xb, gaw_ref[...], preferred_element_type=F32) + gab_ref[...])
    log_sig_lam = -_softplus(-lam_ref[...])
    log_a = RG_C * r_t * log_sig_lam
    a = jnp.exp(log_a)
    th = jnp.tanh(log_a)
    mult = jnp.sqrt(jnp.maximum(-2.0 * th / (1.0 - th), 0.0))
    b = mult * (i_t * xc)

    acs, bcs = [a[0:ng]], [b[0:ng]]
    for p in range(1, nph):
        ap, bp = a[ng * p:ng * (p + 1)], b[ng * p:ng * (p + 1)]
        acs.append(ap * acs[-1])
        bcs.append(ap * bcs[-1] + bp)
    pa, pb = acs[-1], bcs[-1]
    s = 1
    while s < ng:
        valid = grow >= s
        pb = jnp.where(valid, pa * pltpu.roll(pb, s, 0) + pb, pb)
        pa = jnp.where(valid, pa * pltpu.roll(pa, s, 0), pa)
        s *= 2
    hc = hc_ref[...]
    hend = pa * hc + pb
    hin = prev_group(hend, hc)
    hc_ref[...] = hend[ng - 1:ng]
    for p in range(nph):
        yp = (acs[p] * hin + bcs[p]) * phase(gate_a, gate_b, p)
        ys_a[pl.ds(p, ng, stride=nph), :] = yp[:, :LANES]
        ys_b[pl.ds(p, ng, stride=nph), :] = yp[:, LANES:]
    y_ref[...] = jnp.concatenate([ys_a[...], ys_b[...]], axis=1).astype(y_ref.dtype)


def rg_scan(rec, gate, conv_w, conv_b, gx_w, gx_b, ga_w, ga_b, lam, layer, batch, tt=512):
    t, d = rec.shape
    seq = t // batch
    nc = seq // tt
    blk = d // RG_HEADS
    assert blk == 2 * LANES
    act_spec = pl.BlockSpec((tt, blk), lambda b, h, c: (b * nc + c, h))
    half_a = pl.BlockSpec((tt, LANES), lambda b, h, c: (b * nc + c, 2 * h))
    half_b = pl.BlockSpec((tt, LANES), lambda b, h, c: (b * nc + c, 2 * h + 1))
    vec_spec = pl.BlockSpec((1, blk), lambda b, h, c: (0, h))
    w_spec = pl.BlockSpec((None, None, blk, blk), lambda b, h, c: (layer, h, 0, 0))
    return pl.pallas_call(
        functools.partial(_rg_scan_kernel, tt=tt),
        out_shape=jax.ShapeDtypeStruct((t, d), BF16),
        grid=(batch, RG_HEADS, nc),
        in_specs=[half_a, half_b, half_a, half_b,
                  pl.BlockSpec((CONV_W, blk), lambda b, h, c: (0, h)), vec_spec,
                  w_spec, vec_spec, w_spec, vec_spec, vec_spec],
        out_specs=act_spec,
        scratch_shapes=[pltpu.VMEM((SUBLANES, blk), F32), pltpu.VMEM((1, blk), F32),
                        pltpu.VMEM((tt, LANES), F32), pltpu.VMEM((tt, LANES), F32)],
        compiler_params=_params(("parallel", "parallel", "arbitrary")),
        name="rg_scan",
    )(rec, rec, gate, gate, conv_w, conv_b.reshape(1, d), gx_w, gx_b.reshape(1, d),
      ga_w, ga_b.reshape(1, d), lam.reshape(1, d))


PREV_ROWS = 16


def _prev_spec(tm, d, n_lead):
    def index(*ids):
        return (jnp.maximum(ids[n_lead] * (tm // PREV_ROWS) - 1, 0), 0)
    return pl.BlockSpec((PREV_ROWS, d), index)


def _token_shift_delta(x, prev_ref, is_first):
    prev_last = prev_ref[...].astype(F32)[PREV_ROWS - 1:PREV_ROWS, :]
    prev_last = jnp.where(is_first, 0.0, prev_last)
    row = lax.broadcasted_iota(jnp.int32, x.shape, 0)
    xs = jnp.where(row == 0, prev_last, pltpu.roll(x, 1, 0))
    return xs - x


def _rw_proj_kernel(x_ref, prev_ref, mu_ref, w_ref, o_ref, *, blocks_per_seq):
    i = pl.program_id(1)
    x = x_ref[...].astype(F32)
    xx = _token_shift_delta(x, prev_ref, (i % blocks_per_seq) == 0)
    xm = (x + xx * mu_ref[...]).astype(BF16)
    o_ref[...] = jnp.dot(xm, w_ref[...], preferred_element_type=F32)


def rw_proj(hn, mu, w_rkv, layer, seq, tm=512):
    t, d = hn.shape
    nmat = w_rkv.shape[1]
    return pl.pallas_call(
        functools.partial(_rw_proj_kernel, blocks_per_seq=seq // tm),
        out_shape=jax.ShapeDtypeStruct((nmat, t, d), F32),
        grid=(nmat, t // tm),
        in_specs=[pl.BlockSpec((tm, d), lambda j, i: (i, 0)),
                  _prev_spec(tm, d, 1),
                  pl.BlockSpec((None, 1, d), lambda j, i: (j, 0, 0)),
                  pl.BlockSpec((None, None, d, d), lambda j, i: (layer, j, 0, 0))],
        out_specs=pl.BlockSpec((None, tm, d), lambda j, i: (j, i, 0)),
        compiler_params=_params(("parallel", "parallel")),
        name="rw_proj",
    )(hn, hn, mu.reshape(nmat, 1, d), w_rkv)


def _rw_lora_kernel(x_ref, prev_ref, mu_ref, w0_ref, w1_ref, w2_ref, a0_ref, a1_ref, a2_ref,
                    g1_ref, g2_ref, lw_ref, a_ref, g_ref, *, blocks_per_seq):
    i = pl.program_id(0)
    x = x_ref[...].astype(F32)
    xx = _token_shift_delta(x, prev_ref, (i % blocks_per_seq) == 0)
    xw = (x + xx * mu_ref[0:1, :]).astype(BF16)
    xa = (x + xx * mu_ref[1:2, :]).astype(BF16)
    xg = (x + xx * mu_ref[2:3, :]).astype(BF16)
    hid = jnp.tanh(jnp.dot(xw, w1_ref[...], preferred_element_type=F32))
    w = w0_ref[...] + _dot(hid, w2_ref[...])
    lw_ref[...] = -0.6065306597126334 * jax.nn.sigmoid(w)
    hid = jnp.dot(xa, a1_ref[...], preferred_element_type=F32)
    a_ref[...] = jax.nn.sigmoid(a0_ref[...] + _dot(hid, a2_ref[...]))
    hid = jax.nn.sigmoid(jnp.dot(xg, g1_ref[...], preferred_element_type=F32))
    g_ref[...] = _dot(hid, g2_ref[...])


def rw_lora(hn, mu3, w0, w1, w2, a0, a1, a2, g1, g2, seq, tm=512):
    t, d = hn.shape
    row_spec = pl.BlockSpec((tm, d), lambda i: (i, 0))

    def full(arr):
        return pl.BlockSpec(arr.shape, lambda i: (0,) * arr.ndim)

    w0, a0 = w0.reshape(1, d), a0.reshape(1, d)
    consts = (mu3, w0, w1, w2, a0, a1, a2, g1, g2)
    return pl.pallas_call(
        functools.partial(_rw_lora_kernel, blocks_per_seq=seq // tm),
        out_shape=[jax.ShapeDtypeStruct((t, d), F32)] * 3,
        grid=(t // tm,),
        in_specs=[row_spec, _prev_spec(tm, d, 0)] + [full(c) for c in consts],
        out_specs=[row_spec] * 3,
        compiler_params=_params(("parallel",)),
        name="rw_lora",
    )(hn, hn, *consts)


def _split_terms(x, passes):
    terms = []
    for i in range(passes):
        part = x.astype(BF16)
        terms.append(part)
        if i + 1 < passes:
            x = x - part.astype(F32)
    return terms


def _unit_lower_inverse_many(lps, row, col, eye, mul):
    diag8 = jnp.right_shift(row, 3) == jnp.right_shift(col, 3)
    lds = [jnp.where(diag8, lp, 0.0) for lp in lps]
    l2s = [mul(ld, ld) for ld in lds]
    l4s = [mul(l2, l2) for l2 in l2s]
    ts = [mul(eye - ld, eye + l2) for ld, l2 in zip(lds, l2s)]
    ts = [mul(t, eye + l4) for t, l4 in zip(ts, l4s)]
    for sh in (3, 4, 5):
        rb, cb = jnp.right_shift(row, sh), jnp.right_shift(col, sh)
        sel = (jnp.right_shift(rb, 1) == jnp.right_shift(cb, 1)) & (rb != cb)
        ots = [mul(jnp.where(sel, lp, 0.0), t) for lp, t in zip(lps, ts)]
        ts = [t - mul(t, ot) for t, ot in zip(ts, ots)]
    return ts


def _rw_rec_kernel(r_ref, k_ref, v_ref, lw_ref, a_ref, g_ref, kk_ref, ka_ref, rk_ref,
                   lng_ref, lnb_ref, o_ref, s_ref, y_ref, *, n_chunks, n_pairs):
    c = pl.program_id(2)

    @pl.when(c == 0)
    def _():
        s_ref[...] = jnp.zeros_like(s_ref)

    tt, lwid = n_chunks * CHUNK, n_pairs * LANES
    n2 = 2 * CHUNK
    shift = CHUNK.bit_length() - 1
    row = lax.broadcasted_iota(jnp.int32, (CHUNK, n2), 0)
    col = jnp.bitwise_and(lax.broadcasted_iota(jnp.int32, (CHUNK, n2), 1), CHUNK - 1)
    strict = row > col
    incl = row >= col
    eye = (row == col).astype(F32)
    trow = lax.broadcasted_iota(jnp.int32, (tt, tt), 0)
    tcol = lax.broadcasted_iota(jnp.int32, (tt, tt), 1)
    tri = ((jnp.right_shift(trow, shift) == jnp.right_shift(tcol, shift))
           & (trow >= tcol)).astype(BF16)
    hw = min(lwid, MXU_DIM)
    hrow = lax.broadcasted_iota(jnp.int32, (hw, hw), 0)
    hcol = lax.broadcasted_iota(jnp.int32, (hw, hw), 1)
    head_ones = (jnp.right_shift(hrow, shift) == jnp.right_shift(hcol, shift)).astype(BF16)
    same_head = (jnp.right_shift(lax.broadcasted_iota(jnp.int32, (n2, n2), 0), shift)
                 == jnp.right_shift(lax.broadcasted_iota(jnp.int32, (n2, n2), 1), shift)).astype(F32)
    lane = lax.broadcasted_iota(jnp.int32, (CHUNK, n2), 1)
    m0b = (lane < RW_HEAD).astype(BF16)
    m1b = (lane >= RW_HEAD).astype(BF16)

    def head_sum(x, passes):
        cols = []
        for c0 in range(0, lwid, hw):
            cols.append(sum(jnp.dot(p, head_ones, preferred_element_type=F32)
                            for p in _split_terms(x[:, c0:c0 + hw], passes)))
        return jnp.concatenate(cols, axis=1)

    def stack_b(x):
        xb = x.astype(BF16)
        return jnp.concatenate([xb * m0b, xb * m1b], axis=0)

    r, k, v = r_ref[...], k_ref[...], v_ref[...]
    lw, a = lw_ref[...], a_ref[...]
    kkp = k * kk_ref[...]
    kap = kkp * lax.rsqrt(jnp.maximum(head_sum(kkp * kkp, 1), 1e-24))
    b = kap * a
    km = k * (1.0 + (a - 1.0) * ka_ref[...])
    cum = sum(jnp.dot(tri, p, preferred_element_type=F32) for p in _split_terms(lw, 3))
    e_in = jnp.exp(cum)
    e_neg = jnp.exp(-cum)
    kt = kap * jnp.exp(cum - lw)
    bt = b * e_neg
    kmt = km * e_neg
    rt = r * e_in

    items = [(ci, p) for ci in range(n_chunks) for p in range(n_pairs)]

    def blk(x, ci, p):
        return x[CHUNK * ci:CHUNK * (ci + 1), LANES * p:LANES * (p + 1)]

    def mul(a_sb, b_sb):
        return jnp.dot(a_sb.astype(BF16), stack_b(b_sb), preferred_element_type=F32)

    ktn = [blk(kt, *it) for it in items]
    rtn = [blk(rt, *it) for it in items]
    kts = [stack_b(x) for x in ktn]
    bts = [stack_b(blk(bt, *it)) for it in items]
    kms = [stack_b(blk(kmt, *it)) for it in items]
    vn = [blk(v, *it) for it in items]
    vsb = [stack_b(x) for x in vn]
    kbn = [jnp.concatenate([blk(kmt, *it), blk(bt, *it)], axis=0).astype(BF16) for it in items]
    pcs = [blk(e_in, *it)[CHUNK - 1:CHUNK, :] for it in items]
    kbs = [jnp.concatenate([km_, bt_], axis=0) for km_, bt_ in zip(kms, bts)]
    amats = [_dot_nt(jnp.concatenate([kt_, rt_], axis=0), kb)
             for kt_, rt_, kb in zip(ktn, rtn, kbs)]
    lps = [jnp.where(strict, am[:CHUNK, n2:], 0.0) for am in amats]
    mps = [jnp.where(strict, am[:CHUNK, :n2], 0.0).astype(BF16) for am in amats]
    arbs = [jnp.where(incl, am[CHUNK:, n2:], 0.0).astype(BF16) for am in amats]
    arks = [jnp.where(incl, am[CHUNK:, :n2], 0.0).astype(BF16) for am in amats]
    tps = [t.astype(BF16) for t in _unit_lower_inverse_many(lps, row, col, eye, mul)]
    mvs = [jnp.dot(mp, v_, preferred_element_type=F32) for mp, v_ in zip(mps, vsb)]
    wus = [jnp.dot(tp, jnp.concatenate([kt_, stack_b(mv)], axis=1),
                   preferred_element_type=F32)
           for tp, kt_, mv in zip(tps, kts, mvs)]
    awus = [jnp.dot(arb, jnp.concatenate([stack_b(wu[:, :n2]), stack_b(wu[:, n2:])], axis=1),
                    preferred_element_type=F32)
            for arb, wu in zip(arbs, wus)]
    qs = [(rt_ - awu[:, :n2]).astype(BF16) for rt_, awu in zip(rtn, awus)]
    ylocs = [jnp.dot(ark, v_, preferred_element_type=F32) - awu[:, n2:]
             for ark, v_, awu in zip(arks, vsb, awus)]
    gms = [(-_dot(wu[:, :n2].T, kb[CHUNK:]) * (same_head * pc)).astype(BF16)
           for wu, kb, pc in zip(wus, kbn, pcs)]
    hms = [_dot(jnp.concatenate([v_, -wu[:, n2:]], axis=0).T, kb) * (same_head * pc)
           for v_, wu, kb, pc in zip(vn, wus, kbn, pcs)]

    states = [s_ref[p] for p in range(n_pairs)]
    for idx, (ci, p) in enumerate(items):
        sb = states[p].astype(BF16)
        y_ref[CHUNK * ci:CHUNK * (ci + 1), LANES * p:LANES * (p + 1)] = (
            _dot_nt(qs[idx], sb) + ylocs[idx])
        states[p] = (states[p] * pcs[idx] + jnp.dot(sb, gms[idx], preferred_element_type=F32)
                     + hms[idx])
    for p in range(n_pairs):
        s_ref[p] = states[p]

    y = y_ref[...]
    dlt = y - head_sum(y, 2) * (1.0 / RW_HEAD)
    var = head_sum(dlt * dlt, 2) * (1.0 / RW_HEAD)
    yn = dlt * lax.rsqrt(var + GN_EPS) * lng_ref[...] + lnb_ref[...]
    bonus = head_sum(r * km * rk_ref[...], 1) * v
    o_ref[...] = ((yn + bonus) * g_ref[...]).astype(o_ref.dtype)


def rw_rec(rkv, lw, a, g, k_k, k_a, r_k, ln_g, ln_b, batch, n_chunks=4, n_pairs=4):
    _, t, d = rkv.shape
    seq = t // batch
    tt = n_chunks * CHUNK
    lwid = n_pairs * LANES
    nc = seq // tt

    def rkv_spec(m):
        return pl.BlockSpec((None, tt, lwid), lambda b, l, c: (m, b * nc + c, l))

    act_spec = pl.BlockSpec((tt, lwid), lambda b, l, c: (b * nc + c, l))
    vec_spec = pl.BlockSpec((1, lwid), lambda b, l, c: (0, l))
    vecs = [x.reshape(1, d) for x in (k_k, k_a, r_k, ln_g, ln_b)]
    return pl.pallas_call(
        functools.partial(_rw_rec_kernel, n_chunks=n_chunks, n_pairs=n_pairs),
        out_shape=jax.ShapeDtypeStruct((t, d), BF16),
        grid=(batch, d // lwid, nc),
        in_specs=[rkv_spec(0), rkv_spec(1), rkv_spec(2), act_spec, act_spec, act_spec]
                 + [vec_spec] * 5,
        out_specs=act_spec,
        scratch_shapes=[pltpu.VMEM((n_pairs, LANES, LANES), F32), pltpu.VMEM((tt, lwid), F32)],
        compiler_params=_params(("parallel", "parallel", "arbitrary")),
        name="rw_rec",
    )(rkv, rkv, rkv, lw, a, g, *vecs)


def _pad_axis(x, axis, size):
    pad = [(0, 0)] * x.ndim
    pad[axis] = (0, size - x.shape[axis])
    return jnp.pad(x, pad)


def kernel(x, norm_mix_g, norm_mlp_g, w_mlp_up, w_mlp_down, final_norm_g, rg_w_in, rg_conv_w, rg_conv_b, rg_gx_w, rg_gx_b, rg_ga_w, rg_ga_b, rg_lambda, rg_w_out, rw_mu, rw_w_rkv, rw_w0, rw_w1, rw_w2, rw_a0, rw_a1, rw_a2, rw_g1, rw_g2, rw_k_k, rw_k_a, rw_r_k, rw_ln_g, rw_ln_b, rw_w_out):
    batch, seq, d = x.shape
    depth = norm_mix_g.shape[0]
    t = batch * seq
    h = x.reshape(t, d)
    rg_w_out_b, rw_w_out_b = rg_w_out.astype(BF16), rw_w_out.astype(BF16)
    rg_gx_b16, rg_ga_b16 = rg_gx_w.astype(BF16), rg_ga_w.astype(BF16)
    rw_w_rkv_b = rw_w_rkv.astype(BF16)
    hn = rmsnorm(h, norm_mix_g[0], BF16)
    for i in range(depth):
        j = i // 2
        if i % 2 == 0:
            gate = mm_act(hn, rg_w_in, j, "gelu", F32, col0=0, n=d)
            rec = mm_act(hn, rg_w_in, j, "none", F32, col0=d, n=d)
            y = rg_scan(rec, gate, rg_conv_w[j], rg_conv_b[j], rg_gx_b16, rg_gx_b[j],
                        rg_ga_b16, rg_ga_b[j], rg_lambda[j], j, batch)
            w_out = rg_w_out_b
        else:
            rkv = rw_proj(hn, rw_mu[j, :3], rw_w_rkv_b, j, seq)
            lora = LANES
            lw, a, g = rw_lora(
                hn, rw_mu[j, 3:], rw_w0[j],
                _pad_axis(rw_w1[j], 1, lora).astype(BF16), _pad_axis(rw_w2[j], 0, lora).astype(BF16),
                rw_a0[j],
                _pad_axis(rw_a1[j], 1, lora).astype(BF16), _pad_axis(rw_a2[j], 0, lora).astype(BF16),
                rw_g1[j].astype(BF16), rw_g2[j].astype(BF16), seq)
            y = rw_rec(rkv, lw, a, g, rw_k_k[j], rw_k_a[j], rw_r_k[j].reshape(d),
                       rw_ln_g[j], rw_ln_b[j], batch)
            w_out = rw_w_out_b
        h, hn = mm_res_norm(y, w_out, j, h, norm_mlp_g[i], BF16, tk=d)
        u, w_down_b = mm_act(hn, w_mlp_up, i, "relu2", BF16, cast=w_mlp_down)
        last = i + 1 == depth
        g_next = final_norm_g if last else norm_mix_g[i + 1]
        h, hn = mm_res_norm(u, w_down_b, None, h, g_next, F32 if last else BF16, emit_h=not last,
                            tm=1024, tk=512)
    return hn.reshape(batch, seq, d)
```

```python
import functools

import jax
import jax.numpy as jnp
from jax import lax
from jax.experimental import pallas as pl
from jax.experimental.pallas import tpu as pltpu

F32 = jnp.float32
BF16 = jnp.bfloat16

RMS_EPS = 1e-6
GN_EPS = 64e-5
RG_C = 8.0
RG_HEADS = 8
CONV_W = 4
RW_HEAD = 64
LANES = 128
MXU_DIM = 256
SUBLANES = 8
CHUNK = 64
VMEM_LIMIT = 56 * 1024 * 1024


def _params(sem):
    return pltpu.CompilerParams(dimension_semantics=sem, vmem_limit_bytes=VMEM_LIMIT)


def _dot(a, b):
    return jnp.dot(a.astype(BF16), b.astype(BF16), preferred_element_type=F32)


def _dot_nt(a, b):
    return lax.dot_general(a.astype(BF16), b.astype(BF16), (((1,), (1,)), ((), ())),
                           preferred_element_type=F32)


def _gelu_tanh(x):
    return 0.5 * x * (1.0 + jnp.tanh(0.7978845608028654 * (x + 0.044715 * (x * x * x))))


def _softplus(x):
    return jnp.maximum(x, 0.0) + jnp.log1p(jnp.exp(-jnp.abs(x)))


def _rmsnorm_kernel(x_ref, g_ref, o_ref):
    x = x_ref[...]
    ms = jnp.mean(x * x, axis=-1, keepdims=True)
    o_ref[...] = (x * lax.rsqrt(ms + RMS_EPS) * g_ref[...]).astype(o_ref.dtype)


def rmsnorm(x, g, out_dtype, tm=512):
    t, d = x.shape
    return pl.pallas_call(
        _rmsnorm_kernel,
        out_shape=jax.ShapeDtypeStruct((t, d), out_dtype),
        grid=(t // tm,),
        in_specs=[pl.BlockSpec((tm, d), lambda i: (i, 0)),
                  pl.BlockSpec((1, d), lambda i: (0, 0))],
        out_specs=pl.BlockSpec((tm, d), lambda i: (i, 0)),
        compiler_params=_params(("parallel",)),
        name="rmsnorm",
    )(x, g.reshape(1, d))


def _mm_act_kernel(a_ref, w_ref, *refs, act, with_cast):
    if with_cast:
        c_ref, o_ref, cb_ref, wb_ref = refs
        cb_ref[...] = c_ref[...].astype(BF16)
    else:
        o_ref, wb_ref = refs

    @pl.when(pl.program_id(1) == 0)
    def _():
        wb_ref[...] = w_ref[...].astype(BF16)

    y = jnp.dot(a_ref[...], wb_ref[...], preferred_element_type=F32)
    if act == "relu2":
        y = jnp.square(jnp.maximum(y, 0.0))
    elif act == "gelu":
        y = _gelu_tanh(y)
    o_ref[...] = y.astype(o_ref.dtype)


def mm_act(a, w, layer, act, out_dtype, col0=0, n=None, cast=None, tm=1024, tn=1024):
    t, k = a.shape
    n = w.shape[2] if n is None else n
    j0 = col0 // tn
    nj, ni = n // tn, t // tm
    in_specs = [pl.BlockSpec((tm, k), lambda j, i: (i, 0)),
                pl.BlockSpec((None, k, tn), lambda j, i: (layer, 0, j0 + j))]
    out_shape = [jax.ShapeDtypeStruct((t, n), out_dtype)]
    out_specs = [pl.BlockSpec((tm, tn), lambda j, i: (i, j))]
    args = [a, w]
    if cast is not None:
        _, rows, cols = cast.shape
        slab = rows // (nj * ni)
        in_specs.append(pl.BlockSpec((None, slab, cols), lambda j, i: (layer, j * ni + i, 0)))
        out_shape.append(jax.ShapeDtypeStruct((rows, cols), BF16))
        out_specs.append(pl.BlockSpec((slab, cols), lambda j, i: (j * ni + i, 0)))
        args.append(cast)
    res = pl.pallas_call(
        functools.partial(_mm_act_kernel, act=act, with_cast=cast is not None),
        out_shape=out_shape,
        grid=(nj, ni),
        in_specs=in_specs,
        out_specs=out_specs,
        scratch_shapes=[pltpu.VMEM((k, tn), BF16)],
        compiler_params=_params(("parallel", "arbitrary")),
        name="mm_" + act,
    )(*args)
    return res if cast is not None else res[0]


def _mm_res_norm_kernel(a_ref, w_ref, h_ref, g_ref, *refs, emit_h, nk):
    refs = list(refs)
    hnew_ref = refs.pop(0) if emit_h else None
    hn_ref = refs.pop(0)
    kk = pl.program_id(1)

    def part():
        return jnp.dot(a_ref[...], w_ref[...], preferred_element_type=F32)

    def norm(hnew):
        ms = jnp.mean(hnew * hnew, axis=-1, keepdims=True)
        hn_ref[...] = (hnew * lax.rsqrt(ms + RMS_EPS) * g_ref[...]).astype(hn_ref.dtype)

    if nk == 1:
        hnew = h_ref[...] + part()
        if emit_h:
            hnew_ref[...] = hnew
        norm(hnew)
        return
    acc_ref = hnew_ref if emit_h else refs.pop(0)

    @pl.when(kk == 0)
    def _():
        acc_ref[...] = h_ref[...] + part()

    @pl.when(kk > 0)
    def _():
        acc_ref[...] += part()

    @pl.when(kk == nk - 1)
    def _():
        norm(acc_ref[...])


def mm_res_norm(a, w, layer, h, g, hn_dtype, emit_h=True, tm=512, tk=1024):
    t, k = a.shape
    d = w.shape[-1]
    nk = k // tk
    if layer is None:
        w_spec = pl.BlockSpec((tk, d), lambda i, kk: (kk, 0))
    else:
        w_spec = pl.BlockSpec((None, tk, d), lambda i, kk: (layer, kk, 0))
    need_scratch = nk > 1 and not emit_h
    row_spec = pl.BlockSpec((tm, d), lambda i, kk: (i, 0))
    out_shape = [jax.ShapeDtypeStruct((t, d), hn_dtype)]
    out_specs = [row_spec]
    if emit_h:
        out_shape = [jax.ShapeDtypeStruct((t, d), F32)] + out_shape
        out_specs = [row_spec] + out_specs
    res = pl.pallas_call(
        functools.partial(_mm_res_norm_kernel, emit_h=emit_h, nk=nk),
        out_shape=out_shape,
        grid=(t // tm, nk),
        in_specs=[pl.BlockSpec((tm, tk), lambda i, kk: (i, kk)),
                  w_spec,
                  row_spec,
                  pl.BlockSpec((1, d), lambda i, kk: (0, 0))],
        out_specs=out_specs,
        scratch_shapes=[pltpu.VMEM((tm, d), F32)] if need_scratch else [],
        compiler_params=_params(("parallel", "arbitrary")),
        name="mm_res_norm",
    )(a, w, h, g.reshape(1, d))
    if emit_h:
        return res[0], res[1]
    return None, res[0]


def _rg_scan_kernel(rec_a, rec_b, gate_a, gate_b, cw_ref, cb_ref, gxw_ref, gxb_ref, gaw_ref,
                    gab_ref, lam_ref, y_ref, tail_ref, hc_ref, ys_a, ys_b, *, tt):
    c = pl.program_id(2)
    nph = SUBLANES
    ng = tt // nph

    @pl.when(c == 0)
    def _():
        tail_ref[...] = jnp.zeros_like(tail_ref)
        hc_ref[...] = jnp.zeros_like(hc_ref)

    def phase(ref_a, ref_b, p):
        rows = pl.ds(p, ng, stride=nph)
        return jnp.concatenate([ref_a[rows, :], ref_b[rows, :]], axis=1)

    grow = lax.broadcasted_iota(jnp.int32, (ng, 2 * LANES), 0)

    def prev_group(arr, first):
        return jnp.where(grow == 0, first, pltpu.roll(arr, 1, 0))

    xs = [phase(rec_a, rec_b, p) for p in range(nph)]
    tail = tail_ref[...]
    xprev = {p: prev_group(xs[p], tail[p:p + 1]) for p in range(nph - CONV_W + 1, nph)}
    tail_ref[...] = jnp.concatenate([rec_a[tt - nph:tt, :], rec_b[tt - nph:tt, :]], axis=1)
    xcs = []
    for p in range(nph):
        acc = cb_ref[...] + cw_ref[CONV_W - 1:CONV_W, :] * xs[p]
        for j in range(1, CONV_W):
            src = xs[p - j] if p >= j else xprev[p - j + nph]
            acc = acc + cw_ref[CONV_W - 1 - j:CONV_W - j, :] * src
        xcs.append(acc)
    xc = jnp.concatenate(xcs, axis=0)

    xb = xc.astype(BF16)
    i_t = jax.nn.sigmoid(jnp.dot(xb, gxw_ref[...], preferred_element_type=F32) + gxb_ref[...])
    r_t = jax.nn.sigmoid(jnp.dot(xb, gaw_ref[...], preferred_element_type=F32) + gab_ref[...])
    log_sig_lam = -_softplus(-lam_ref[...])
    log_a = RG_C * r_t * log_sig_lam
    a = jnp.exp(log_a)
    th = jnp.tanh(log_a)
    mult = jnp.sqrt(jnp.maximum(-2.0 * th / (1.0 - th), 0.0))
    b = mult * (i_t * xc)

    acs, bcs = [a[0:ng]], [b[0:ng]]
    for p in range(1, nph):
        ap, bp = a[ng * p:ng * (p + 1)], b[ng * p:ng * (p + 1)]
        acs.append(ap * acs[-1])
        bcs.append(ap * bcs[-1] + bp)
    pa, pb = acs[-1], bcs[-1]
    s = 1
    while s < ng:
        valid = grow >= s
        pb = jnp.where(valid, pa * pltpu.roll(pb, s, 0) + pb, pb)
        pa = jnp.where(valid, pa * pltpu.roll(pa, s, 0), pa)
        s *= 2
    hc = hc_ref[...]
    hend = pa * hc + pb
    hin = prev_group(hend, hc)
    hc_ref[...] = hend[ng - 1:ng]
    for p in range(nph):
        yp = (acs[p] * hin + bcs[p]) * phase(gate_a, gate_b, p)
        ys_a[pl.ds(p, ng, stride=nph), :] = yp[:, :LANES]
        ys_b[pl.ds(p, ng, stride=nph), :] = yp[:, LANES:]
    y_ref[...] = jnp.concatenate([ys_a[...], ys_b[...]], axis=1).astype(y_ref.dtype)


def rg_scan(rec, gate, conv_w, conv_b, gx_w, gx_b, ga_w, ga_b, lam, layer, batch, tt=1024):
    t, d = rec.shape
    seq = t // batch
    nc = seq // tt
    blk = d // RG_HEADS
    assert blk == 2 * LANES
    act_spec = pl.BlockSpec((tt, blk), lambda b, h, c: (b * nc + c, h))
    half_a = pl.BlockSpec((tt, LANES), lambda b, h, c: (b * nc + c, 2 * h))
    half_b = pl.BlockSpec((tt, LANES), lambda b, h, c: (b * nc + c, 2 * h + 1))
    vec_spec = pl.BlockSpec((1, blk), lambda b, h, c: (0, h))
    w_spec = pl.BlockSpec((None, None, blk, blk), lambda b, h, c: (layer, h, 0, 0))
    return pl.pallas_call(
        functools.partial(_rg_scan_kernel, tt=tt),
        out_shape=jax.ShapeDtypeStruct((t, d), BF16),
        grid=(batch, RG_HEADS, nc),
        in_specs=[half_a, half_b, half_a, half_b,
                  pl.BlockSpec((CONV_W, blk), lambda b, h, c: (0, h)), vec_spec,
                  w_spec, vec_spec, w_spec, vec_spec, vec_spec],
        out_specs=act_spec,
        scratch_shapes=[pltpu.VMEM((SUBLANES, blk), F32), pltpu.VMEM((1, blk), F32),
                        pltpu.VMEM((tt, LANES), F32), pltpu.VMEM((tt, LANES), F32)],
        compiler_params=_params(("parallel", "parallel", "arbitrary")),
        name="rg_scan",
    )(rec, rec, gate, gate, conv_w, conv_b.reshape(1, d), gx_w, gx_b.reshape(1, d),
      ga_w, ga_b.reshape(1, d), lam.reshape(1, d))


PREV_ROWS = 16


def _prev_spec(tm, d, n_lead):
    def index(*ids):
        return (jnp.maximum(ids[n_lead] * (tm // PREV_ROWS) - 1, 0), 0)
    return pl.BlockSpec((PREV_ROWS, d), index)


def _token_shift_delta(x, prev_ref, is_first):
    prev_last = prev_ref[...].astype(F32)[PREV_ROWS - 1:PREV_ROWS, :]
    prev_last = jnp.where(is_first, 0.0, prev_last)
    row = lax.broadcasted_iota(jnp.int32, x.shape, 0)
    xs = jnp.where(row == 0, prev_last, pltpu.roll(x, 1, 0))
    return xs - x


def _rw_proj_kernel(x_ref, prev_ref, mu_ref, w_ref, o_ref, *, blocks_per_seq):
    i = pl.program_id(1)
    x = x_ref[...].astype(F32)
    xx = _token_shift_delta(x, prev_ref, (i % blocks_per_seq) == 0)
    xm = (x + xx * mu_ref[...]).astype(BF16)
    o_ref[...] = jnp.dot(xm, w_ref[...], preferred_element_type=F32)


def rw_proj(hn, mu, w_rkv, layer, seq, tm=1024):
    t, d = hn.shape
    nmat = w_rkv.shape[1]
    return pl.pallas_call(
        functools.partial(_rw_proj_kernel, blocks_per_seq=seq // tm),
        out_shape=jax.ShapeDtypeStruct((nmat, t, d), F32),
        grid=(nmat, t // tm),
        in_specs=[pl.BlockSpec((tm, d), lambda j, i: (i, 0)),
                  _prev_spec(tm, d, 1),
                  pl.BlockSpec((None, 1, d), lambda j, i: (j, 0, 0)),
                  pl.BlockSpec((None, None, d, d), lambda j, i: (layer, j, 0, 0))],
        out_specs=pl.BlockSpec((None, tm, d), lambda j, i: (j, i, 0)),
        compiler_params=_params(("parallel", "parallel")),
        name="rw_proj",
    )(hn, hn, mu.reshape(nmat, 1, d), w_rkv)


def _rw_lora_kernel(x_ref, prev_ref, mu_ref, w0_ref, w1_ref, w2_ref, a0_ref, a1_ref, a2_ref,
                    g1_ref, g2_ref, lw_ref, a_ref, g_ref, *, blocks_per_seq):
    i = pl.program_id(0)
    x = x_ref[...].astype(F32)
    xx = _token_shift_delta(x, prev_ref, (i % blocks_per_seq) == 0)
    xw = (x + xx * mu_ref[0:1, :]).astype(BF16)
    xa = (x + xx * mu_ref[1:2, :]).astype(BF16)
    xg = (x + xx * mu_ref[2:3, :]).astype(BF16)
    hid = jnp.tanh(jnp.dot(xw, w1_ref[...], preferred_element_type=F32))
    w = w0_ref[...] + _dot(hid, w2_ref[...])
    lw_ref[...] = -0.6065306597126334 * jax.nn.sigmoid(w)
    hid = jnp.dot(xa, a1_ref[...], preferred_element_type=F32)
    a_ref[...] = jax.nn.sigmoid(a0_ref[...] + _dot(hid, a2_ref[...]))
    hid = jax.nn.sigmoid(jnp.dot(xg, g1_ref[...], preferred_element_type=F32))
    g_ref[...] = _dot(hid, g2_ref[...])


def rw_lora(hn, mu3, w0, w1, w2, a0, a1, a2, g1, g2, seq, tm=512):
    t, d = hn.shape
    row_spec = pl.BlockSpec((tm, d), lambda i: (i, 0))

    def full(arr):
        return pl.BlockSpec(arr.shape, lambda i: (0,) * arr.ndim)

    w0, a0 = w0.reshape(1, d), a0.reshape(1, d)
    consts = (mu3, w0, w1, w2, a0, a1, a2, g1, g2)
    return pl.pallas_call(
        functools.partial(_rw_lora_kernel, blocks_per_seq=seq // tm),
        out_shape=[jax.ShapeDtypeStruct((t, d), F32)] * 3,
        grid=(t // tm,),
        in_specs=[row_spec, _prev_spec(tm, d, 0)] + [full(c) for c in consts],
        out_specs=[row_spec] * 3,
        compiler_params=_params(("parallel",)),
        name="rw_lora",
    )(hn, hn, *consts)


def _split_terms(x, passes):
    terms = []
    for i in range(passes):
        part = x.astype(BF16)
        terms.append(part)
        if i + 1 < passes:
            x = x - part.astype(F32)
    return terms


def _unit_lower_inverse_many(lps, row, col, eye, mul):
    diag8 = jnp.right_shift(row, 3) == jnp.right_shift(col, 3)
    lds = [jnp.where(diag8, lp, 0.0) for lp in lps]
    l2s = [mul(ld, ld) for ld in lds]
    l4s = [mul(l2, l2) for l2 in l2s]
    ts = [mul(eye - ld, eye + l2) for ld, l2 in zip(lds, l2s)]
    ts = [mul(t, eye + l4) for t, l4 in zip(ts, l4s)]
    for sh in (3, 4, 5):
        rb, cb = jnp.right_shift(row, sh), jnp.right_shift(col, sh)
        sel = (jnp.right_shift(rb, 1) == jnp.right_shift(cb, 1)) & (rb != cb)
        ots = [mul(jnp.where(sel, lp, 0.0), t) for lp, t in zip(lps, ts)]
        ts = [t - mul(t, ot) for t, ot in zip(ts, ots)]
    return ts


def _rw_rec_kernel(r_ref, k_ref, v_ref, lw_ref, a_ref, g_ref, kk_ref, ka_ref, rk_ref,
                   lng_ref, lnb_ref, o_ref, s_ref, y_ref, *, n_chunks, n_pairs):
    c = pl.program_id(2)

    @pl.when(c == 0)
    def _():
        s_ref[...] = jnp.zeros_like(s_ref)

    tt, lwid = n_chunks * CHUNK, n_pairs * LANES
    n2 = 2 * CHUNK
    shift = CHUNK.bit_length() - 1
    row = lax.broadcasted_iota(jnp.int32, (CHUNK, n2), 0)
    col = jnp.bitwise_and(lax.broadcasted_iota(jnp.int32, (CHUNK, n2), 1), CHUNK - 1)
    strict = row > col
    incl = row >= col
    eye = (row == col).astype(F32)
    trow = lax.broadcasted_iota(jnp.int32, (tt, tt), 0)
    tcol = lax.broadcasted_iota(jnp.int32, (tt, tt), 1)
    tri = ((jnp.right_shift(trow, shift) == jnp.right_shift(tcol, shift))
           & (trow >= tcol)).astype(BF16)
    hw = min(lwid, MXU_DIM)
    hrow = lax.broadcasted_iota(jnp.int32, (hw, hw), 0)
    hcol = lax.broadcasted_iota(jnp.int32, (hw, hw), 1)
    head_ones = (jnp.right_shift(hrow, shift) == jnp.right_shift(hcol, shift)).astype(BF16)
    same_head = (jnp.right_shift(lax.broadcasted_iota(jnp.int32, (n2, n2), 0), shift)
                 == jnp.right_shift(lax.broadcasted_iota(jnp.int32, (n2, n2), 1), shift)).astype(F32)
    lane = lax.broadcasted_iota(jnp.int32, (CHUNK, n2), 1)
    m0b = (lane < RW_HEAD).astype(BF16)
    m1b = (lane >= RW_HEAD).astype(BF16)

    def head_sum(x, passes):
        cols = []
        for c0 in range(0, lwid, hw):
            cols.append(sum(jnp.dot(p, head_ones, preferred_element_type=F32)
                            for p in _split_terms(x[:, c0:c0 + hw], passes)))
        return jnp.concatenate(cols, axis=1)

    def stack_b(x):
        xb = x.astype(BF16)
        return jnp.concatenate([xb * m0b, xb * m1b], axis=0)

    r, k, v = r_ref[...], k_ref[...], v_ref[...]
    lw, a = lw_ref[...], a_ref[...]
    kkp = k * kk_ref[...]
    kap = kkp * lax.rsqrt(jnp.maximum(head_sum(kkp * kkp, 1), 1e-24))
    b = kap * a
    km = k * (1.0 + (a - 1.0) * ka_ref[...])
    cum = sum(jnp.dot(tri, p, preferred_element_type=F32) for p in _split_terms(lw, 3))
    e_in = jnp.exp(cum)
    e_neg = jnp.exp(-cum)
    kt = kap * jnp.exp(cum - lw)
    bt = b * e_neg
    kmt = km * e_neg
    rt = r * e_in

    items = [(ci, p) for ci in range(n_chunks) for p in range(n_pairs)]

    def blk(x, ci, p):
        return x[CHUNK * ci:CHUNK * (ci + 1), LANES * p:LANES * (p + 1)]

    def mul(a_sb, b_sb):
        return jnp.dot(a_sb.astype(BF16), stack_b(b_sb), preferred_element_type=F32)

    ktn = [blk(kt, *it) for it in items]
    rtn = [blk(rt, *it) for it in items]
    kts = [stack_b(x) for x in ktn]
    bts = [stack_b(blk(bt, *it)) for it in items]
    kms = [stack_b(blk(kmt, *it)) for it in items]
    vn = [blk(v, *it) for it in items]
    vsb = [stack_b(x) for x in vn]
    kbn = [jnp.concatenate([blk(kmt, *it), blk(bt, *it)], axis=0).astype(BF16) for it in items]
    pcs = [blk(e_in, *it)[CHUNK - 1:CHUNK, :] for it in items]
    kbs = [jnp.concatenate([km_, bt_], axis=0) for km_, bt_ in zip(kms, bts)]
    amats = [_dot_nt(jnp.concatenate([kt_, rt_], axis=0), kb)
             for kt_, rt_, kb in zip(ktn, rtn, kbs)]
    lps = [jnp.where(strict, am[:CHUNK, n2:], 0.0) for am in amats]
    mps = [jnp.where(strict, am[:CHUNK, :n2], 0.0).astype(BF16) for am in amats]
    arbs = [jnp.where(incl, am[CHUNK:, n2:], 0.0).astype(BF16) for am in amats]
    arks = [jnp.where(incl, am[CHUNK:, :n2], 0.0).astype(BF16) for am in amats]
    tps = [t.astype(BF16) for t in _unit_lower_inverse_many(lps, row, col, eye, mul)]
    mvs = [jnp.dot(mp, v_, preferred_element_type=F32) for mp, v_ in zip(mps, vsb)]
    wus = [jnp.dot(tp, jnp.concatenate([kt_, stack_b(mv)], axis=1),
                   preferred_element_type=F32)
           for tp, kt_, mv in zip(tps, kts, mvs)]
    awus = [jnp.dot(arb, jnp.concatenate([stack_b(wu[:, :n2]), stack_b(wu[:, n2:])], axis=1),
                    preferred_element_type=F32)
            for arb, wu in zip(arbs, wus)]
    qs = [(rt_ - awu[:, :n2]).astype(BF16) for rt_, awu in zip(rtn, awus)]
    ylocs = [jnp.dot(ark, v_, preferred_element_type=F32) - awu[:, n2:]
             for ark, v_, awu in zip(arks, vsb, awus)]
    gms = [(-_dot(wu[:, :n2].T, kb[CHUNK:]) * (same_head * pc)).astype(BF16)
           for wu, kb, pc in zip(wus, kbn, pcs)]
    hms = [_dot(jnp.concatenate([v_, -wu[:, n2:]], axis=0).T, kb) * (same_head * pc)
           for v_, wu, kb, pc in zip(vn, wus, kbn, pcs)]

    states = [s_ref[p] for p in range(n_pairs)]
    for idx, (ci, p) in enumerate(items):
        sb = states[p].astype(BF16)
        y_ref[CHUNK * ci:CHUNK * (ci + 1), LANES * p:LANES * (p + 1)] = (
            _dot_nt(qs[idx], sb) + ylocs[idx])
        states[p] = (states[p] * pcs[idx] + jnp.dot(sb, gms[idx], preferred_element_type=F32)
                     + hms[idx])
    for p in range(n_pairs):
        s_ref[p] = states[p]

    y = y_ref[...]
    dlt = y - head_sum(y, 2) * (1.0 / RW_HEAD)
    var = head_sum(dlt * dlt, 2) * (1.0 / RW_HEAD)
    yn = dlt * lax.rsqrt(var + GN_EPS) * lng_ref[...] + lnb_ref[...]
    bonus = head_sum(r * km * rk_ref[...], 1) * v
    o_ref[...] = ((yn + bonus) * g_ref[...]).astype(o_ref.dtype)


def rw_rec(rkv, lw, a, g, k_k, k_a, r_k, ln_g, ln_b, batch, n_chunks=8, n_pairs=4):
    _, t, d = rkv.shape
    seq = t // batch
    tt = n_chunks * CHUNK
    lwid = n_pairs * LANES
    nc = seq // tt

    def rkv_spec(m):
        return pl.BlockSpec((None, tt, lwid), lambda b, l, c: (m, b * nc + c, l))

    act_spec = pl.BlockSpec((tt, lwid), lambda b, l, c: (b * nc + c, l))
    vec_spec = pl.BlockSpec((1, lwid), lambda b, l, c: (0, l))
    vecs = [x.reshape(1, d) for x in (k_k, k_a, r_k, ln_g, ln_b)]
    return pl.pallas_call(
        functools.partial(_rw_rec_kernel, n_chunks=n_chunks, n_pairs=n_pairs),
        out_shape=jax.ShapeDtypeStruct((t, d), BF16),
        grid=(batch, d // lwid, nc),
        in_specs=[rkv_spec(0), rkv_spec(1), rkv_spec(2), act_spec, act_spec, act_spec]
                 + [vec_spec] * 5,
        out_specs=act_spec,
        scratch_shapes=[pltpu.VMEM((n_pairs, LANES, LANES), F32), pltpu.VMEM((tt, lwid), F32)],
        compiler_params=_params(("parallel", "parallel", "arbitrary")),
        name="rw_rec",
    )(rkv, rkv, rkv, lw, a, g, *vecs)


def _pad_axis(x, axis, size):
    pad = [(0, 0)] * x.ndim
    pad[axis] = (0, size - x.shape[axis])
    return jnp.pad(x, pad)


def kernel(x, norm_mix_g, norm_mlp_g, w_mlp_up, w_mlp_down, final_norm_g, rg_w_in, rg_conv_w, rg_conv_b, rg_gx_w, rg_gx_b, rg_ga_w, rg_ga_b, rg_lambda, rg_w_out, rw_mu, rw_w_rkv, rw_w0, rw_w1, rw_w2, rw_a0, rw_a1, rw_a2, rw_g1, rw_g2, rw_k_k, rw_k_a, rw_r_k, rw_ln_g, rw_ln_b, rw_w_out):
    batch, seq, d = x.shape
    depth = norm_mix_g.shape[0]
    t = batch * seq
    h = x.reshape(t, d)
    rg_w_out_b, rw_w_out_b = rg_w_out.astype(BF16), rw_w_out.astype(BF16)
    rg_gx_b16, rg_ga_b16 = rg_gx_w.astype(BF16), rg_ga_w.astype(BF16)
    rw_w_rkv_b = rw_w_rkv.astype(BF16)
    hn = rmsnorm(h, norm_mix_g[0], BF16)
    for i in range(depth):
        j = i // 2
        if i % 2 == 0:
            gate = mm_act(hn, rg_w_in, j, "gelu", F32, col0=0, n=d)
            rec = mm_act(hn, rg_w_in, j, "none", F32, col0=d, n=d)
            y = rg_scan(rec, gate, rg_conv_w[j], rg_conv_b[j], rg_gx_b16, rg_gx_b[j],
                        rg_ga_b16, rg_ga_b[j], rg_lambda[j], j, batch)
            w_out = rg_w_out_b
        else:
            rkv = rw_proj(hn, rw_mu[j, :3], rw_w_rkv_b, j, seq)
            lora = LANES
            lw, a, g = rw_lora(
                hn, rw_mu[j, 3:], rw_w0[j],
                _pad_axis(rw_w1[j], 1, lora).astype(BF16), _pad_axis(rw_w2[j], 0, lora).astype(BF16),
                rw_a0[j],
                _pad_axis(rw_a1[j], 1, lora).astype(BF16), _pad_axis(rw_a2[j], 0, lora).astype(BF16),
                rw_g1[j].astype(BF16), rw_g2[j].astype(BF16), seq)
            y = rw_rec(rkv, lw, a, g, rw_k_k[j], rw_k_a[j], rw_r_k[j].reshape(d),
                       rw_ln_g[j], rw_ln_b[j], batch)
            w_out = rw_w_out_b
        h, hn = mm_res_norm(y, w_out, j, h, norm_mlp_g[i], BF16, tk=d)
        u, w_down_b = mm_act(hn, w_mlp_up, i, "relu2", BF16, cast=w_mlp_down)
        last = i + 1 == depth
        g_next = final_norm_g if last else norm_mix_g[i + 1]
        h, hn = mm_res_norm(u, w_down_b, None, h, g_next, F32 if last else BF16, emit_h=not last,
                            tm=1024, tk=1024)
    return hn.reshape(batch, seq, d)
```

```python
import functools

import jax
import jax.numpy as jnp
from jax import lax
from jax.experimental import pallas as pl
from jax.experimental.pallas import tpu as pltpu

F32 = jnp.float32
BF16 = jnp.bfloat16

RMS_EPS = 1e-6
GN_EPS = 64e-5
RG_C = 8.0
RG_HEADS = 8
CONV_W = 4
RW_HEAD = 64
LANES = 128
MXU_DIM = 256
SUBLANES = 8
CHUNK = 64
VMEM_LIMIT = 56 * 1024 * 1024


def _params(sem):
    return pltpu.CompilerParams(dimension_semantics=sem, vmem_limit_bytes=VMEM_LIMIT)


def _dot(a, b):
    return jnp.dot(a.astype(BF16), b.astype(BF16), preferred_element_type=F32)


def _dot_nt(a, b):
    return lax.dot_general(a.astype(BF16), b.astype(BF16), (((1,), (1,)), ((), ())),
                           preferred_element_type=F32)


def _gelu_tanh(x):
    return 0.5 * x * (1.0 + jnp.tanh(0.7978845608028654 * (x + 0.044715 * (x * x * x))))


def _softplus(x):
    return jnp.maximum(x, 0.0) + jnp.log1p(jnp.exp(-jnp.abs(x)))


def _rmsnorm_kernel(x_ref, g_ref, o_ref):
    x = x_ref[...]
    ms = jnp.mean(x * x, axis=-1, keepdims=True)
    o_ref[...] = (x * lax.rsqrt(ms + RMS_EPS) * g_ref[...]).astype(o_ref.dtype)


def rmsnorm(x, g, out_dtype, tm=512):
    t, d = x.shape
    return pl.pallas_call(
        _rmsnorm_kernel,
        out_shape=jax.ShapeDtypeStruct((t, d), out_dtype),
        grid=(t // tm,),
        in_specs=[pl.BlockSpec((tm, d), lambda i: (i, 0)),
                  pl.BlockSpec((1, d), lambda i: (0, 0))],
        out_specs=pl.BlockSpec((tm, d), lambda i: (i, 0)),
        compiler_params=_params(("parallel",)),
        name="rmsnorm",
    )(x, g.reshape(1, d))


def _mm_act_kernel(a_ref, w_ref, *refs, act, with_cast):
    if with_cast:
        c_ref, o_ref, cb_ref, wb_ref = refs
        cb_ref[...] = c_ref[...].astype(BF16)
    else:
        o_ref, wb_ref = refs

    @pl.when(pl.program_id(1) == 0)
    def _():
        wb_ref[...] = w_ref[...].astype(BF16)

    y = jnp.dot(a_ref[...], wb_ref[...], preferred_element_type=F32)
    if act == "relu2":
        y = jnp.square(jnp.maximum(y, 0.0))
    elif act == "gelu":
        y = _gelu_tanh(y)
    o_ref[...] = y.astype(o_ref.dtype)


def mm_act(a, w, layer, act, out_dtype, col0=0, n=None, cast=None, tm=1024, tn=1024):
    t, k = a.shape
    n = w.shape[2] if n is None else n
    j0 = col0 // tn
    nj, ni = n // tn, t // tm
    in_specs = [pl.BlockSpec((tm, k), lambda j, i: (i, 0)),
                pl.BlockSpec((None, k, tn), lambda j, i: (layer, 0, j0 + j))]
    out_shape = [jax.ShapeDtypeStruct((t, n), out_dtype)]
    out_specs = [pl.BlockSpec((tm, tn), lambda j, i: (i, j))]
    args = [a, w]
    if cast is not None:
        _, rows, cols = cast.shape
        slab = rows // (nj * ni)
        in_specs.append(pl.BlockSpec((None, slab, cols), lambda j, i: (layer, j * ni + i, 0)))
        out_shape.append(jax.ShapeDtypeStruct((rows, cols), BF16))
        out_specs.append(pl.BlockSpec((slab, cols), lambda j, i: (j * ni + i, 0)))
        args.append(cast)
    res = pl.pallas_call(
        functools.partial(_mm_act_kernel, act=act, with_cast=cast is not None),
        out_shape=out_shape,
        grid=(nj, ni),
        in_specs=in_specs,
        out_specs=out_specs,
        scratch_shapes=[pltpu.VMEM((k, tn), BF16)],
        compiler_params=_params(("parallel", "arbitrary")),
        name="mm_" + act,
    )(*args)
    return res if cast is not None else res[0]


def _mm_res_norm_kernel(a_ref, w_ref, h_ref, g_ref, *refs, emit_h, nk):
    refs = list(refs)
    hnew_ref = refs.pop(0) if emit_h else None
    hn_ref = refs.pop(0)
    kk = pl.program_id(1)

    def part():
        return jnp.dot(a_ref[...], w_ref[...], preferred_element_type=F32)

    def norm(hnew):
        ms = jnp.mean(hnew * hnew, axis=-1, keepdims=True)
        hn_ref[...] = (hnew * lax.rsqrt(ms + RMS_EPS) * g_ref[...]).astype(hn_ref.dtype)

    if nk == 1:
        hnew = h_ref[...] + part()
        if emit_h:
            hnew_ref[...] = hnew
        norm(hnew)
        return
    acc_ref = hnew_ref if emit_h else refs.pop(0)

    @pl.when(kk == 0)
    def _():
        acc_ref[...] = h_ref[...] + part()

    @pl.when(kk > 0)
    def _():
        acc_ref[...] += part()

    @pl.when(kk == nk - 1)
    def _():
        norm(acc_ref[...])


def mm_res_norm(a, w, layer, h, g, hn_dtype, emit_h=True, tm=512, tk=1024):
    t, k = a.shape
    d = w.shape[-1]
    nk = k // tk
    if layer is None:
        w_spec = pl.BlockSpec((tk, d), lambda i, kk: (kk, 0))
    else:
        w_spec = pl.BlockSpec((None, tk, d), lambda i, kk: (layer, kk, 0))
    need_scratch = nk > 1 and not emit_h
    row_spec = pl.BlockSpec((tm, d), lambda i, kk: (i, 0))
    out_shape = [jax.ShapeDtypeStruct((t, d), hn_dtype)]
    out_specs = [row_spec]
    if emit_h:
        out_shape = [jax.ShapeDtypeStruct((t, d), F32)] + out_shape
        out_specs = [row_spec] + out_specs
    res = pl.pallas_call(
        functools.partial(_mm_res_norm_kernel, emit_h=emit_h, nk=nk),
        out_shape=out_shape,
        grid=(t // tm, nk),
        in_specs=[pl.BlockSpec((tm, tk), lambda i, kk: (i, kk)),
                  w_spec,
                  row_spec,
                  pl.BlockSpec((1, d), lambda i, kk: (0, 0))],
        out_specs=out_specs,
        scratch_shapes=[pltpu.VMEM((tm, d), F32)] if need_scratch else [],
        compiler_params=_params(("parallel", "arbitrary")),
        name="mm_res_norm",
    )(a, w, h, g.reshape(1, d))
    if emit_h:
        return res[0], res[1]
    return None, res[0]


def _rg_scan_kernel(rec_a, rec_b, gate_a, gate_b, cw_ref, cb_ref, gxw_ref, gxb_ref, gaw_ref,
                    gab_ref, lam_ref, y_ref, tail_ref, hc_ref, ys_a, ys_b, *, tt):
    c = pl.program_id(2)
    nph = SUBLANES
    ng = tt // nph

    @pl.when(c == 0)
    def _():
        tail_ref[...] = jnp.zeros_like(tail_ref)
        hc_ref[...] = jnp.zeros_like(hc_ref)

    def phase(ref_a, ref_b, p):
        rows = pl.ds(p, ng, stride=nph)
        return jnp.concatenate([ref_a[rows, :], ref_b[rows, :]], axis=1)

    grow = lax.broadcasted_iota(jnp.int32, (ng, 2 * LANES), 0)

    def prev_group(arr, first):
        return jnp.where(grow == 0, first, pltpu.roll(arr, 1, 0))

    xs = [phase(rec_a, rec_b, p) for p in range(nph)]
    tail = tail_ref[...]
    xprev = {p: prev_group(xs[p], tail[p:p + 1]) for p in range(nph - CONV_W + 1, nph)}
    tail_ref[...] = jnp.concatenate([rec_a[tt - nph:tt, :], rec_b[tt - nph:tt, :]], axis=1)
    xcs = []
    for p in range(nph):
        acc = cb_ref[...] + cw_ref[CONV_W - 1:CONV_W, :] * xs[p]
        for j in range(1, CONV_W):
            src = xs[p - j] if p >= j else xprev[p - j + nph]
            acc = acc + cw_ref[CONV_W - 1 - j:CONV_W - j, :] * src
        xcs.append(acc)
    xc = jnp.concatenate(xcs, axis=0)

    xb = xc.astype(BF16)
    i_t = jax.nn.sigmoid(jnp.dot(xb, gxw_ref[...], preferred_element_type=F32) + gxb_ref[...])
    r_t = jax.nn.sigmoid(jnp.dot(xb, gaw_ref[...], preferred_element_type=F32) + gab_ref[...])
    log_sig_lam = -_softplus(-lam_ref[...])
    log_a = RG_C * r_t * log_sig_lam
    a = jnp.exp(log_a)
    th = jnp.tanh(log_a)
    mult = jnp.sqrt(jnp.maximum(-2.0 * th / (1.0 - th), 0.0))
    b = mult * (i_t * xc)

    acs, bcs = [a[0:ng]], [b[0:ng]]
    for p in range(1, nph):
        ap, bp = a[ng * p:ng * (p + 1)], b[ng * p:ng * (p + 1)]
        acs.append(ap * acs[-1])
        bcs.append(ap * bcs[-1] + bp)
    pa, pb = acs[-1], bcs[-1]
    s = 1
    while s < ng:
        valid = grow >= s
        pb = jnp.where(valid, pa * pltpu.roll(pb, s, 0) + pb, pb)
        pa = jnp.where(valid, pa * pltpu.roll(pa, s, 0), pa)
        s *= 2
    hc = hc_ref[...]
    hend = pa * hc + pb
    hin = prev_group(hend, hc)
    hc_ref[...] = hend[ng - 1:ng]
    for p in range(nph):
        yp = (acs[p] * hin + bcs[p]) * phase(gate_a, gate_b, p)
        ys_a[pl.ds(p, ng, stride=nph), :] = yp[:, :LANES]
        ys_b[pl.ds(p, ng, stride=nph), :] = yp[:, LANES:]
    y_ref[...] = jnp.concatenate([ys_a[...], ys_b[...]], axis=1).astype(y_ref.dtype)


def rg_scan(rec, gate, conv_w, conv_b, gx_w, gx_b, ga_w, ga_b, lam, layer, batch, tt=1024):
    t, d = rec.shape
    seq = t // batch
    nc = seq // tt
    blk = d // RG_HEADS
    assert blk == 2 * LANES
    act_spec = pl.BlockSpec((tt, blk), lambda b, h, c: (b * nc + c, h))
    half_a = pl.BlockSpec((tt, LANES), lambda b, h, c: (b * nc + c, 2 * h))
    half_b = pl.BlockSpec((tt, LANES), lambda b, h, c: (b * nc + c, 2 * h + 1))
    vec_spec = pl.BlockSpec((1, blk), lambda b, h, c: (0, h))
    w_spec = pl.BlockSpec((None, None, blk, blk), lambda b, h, c: (layer, h, 0, 0))
    return pl.pallas_call(
        functools.partial(_rg_scan_kernel, tt=tt),
        out_shape=jax.ShapeDtypeStruct((t, d), BF16),
        grid=(batch, RG_HEADS, nc),
        in_specs=[half_a, half_b, half_a, half_b,
                  pl.BlockSpec((CONV_W, blk), lambda b, h, c: (0, h)), vec_spec,
                  w_spec, vec_spec, w_spec, vec_spec, vec_spec],
        out_specs=act_spec,
        scratch_shapes=[pltpu.VMEM((SUBLANES, blk), F32), pltpu.VMEM((1, blk), F32),
                        pltpu.VMEM((tt, LANES), F32), pltpu.VMEM((tt, LANES), F32)],
        compiler_params=_params(("parallel", "parallel", "arbitrary")),
        name="rg_scan",
    )(rec, rec, gate, gate, conv_w, conv_b.reshape(1, d), gx_w, gx_b.reshape(1, d),
      ga_w, ga_b.reshape(1, d), lam.reshape(1, d))


PREV_ROWS = 16


def _prev_spec(tm, d, n_lead):
    def index(*ids):
        return (jnp.maximum(ids[n_lead] * (tm // PREV_ROWS) - 1, 0), 0)
    return pl.BlockSpec((PREV_ROWS, d), index)


def _token_shift_delta(x, prev_ref, is_first):
    prev_last = prev_ref[...].astype(F32)[PREV_ROWS - 1:PREV_ROWS, :]
    prev_last = jnp.where(is_first, 0.0, prev_last)
    row = lax.broadcasted_iota(jnp.int32, x.shape, 0)
    xs = jnp.where(row == 0, prev_last, pltpu.roll(x, 1, 0))
    return xs - x


def _rw_proj_kernel(x_ref, prev_ref, mu_ref, w_ref, o_ref, *, blocks_per_seq):
    i = pl.program_id(1)
    x = x_ref[...].astype(F32)
    xx = _token_shift_delta(x, prev_ref, (i % blocks_per_seq) == 0)
    xm = (x + xx * mu_ref[...]).astype(BF16)
    o_ref[...] = jnp.dot(xm, w_ref[...], preferred_element_type=F32)


def rw_proj(hn, mu, w_rkv, layer, seq, tm=1024):
    t, d = hn.shape
    nmat = w_rkv.shape[1]
    return pl.pallas_call(
        functools.partial(_rw_proj_kernel, blocks_per_seq=seq // tm),
        out_shape=jax.ShapeDtypeStruct((nmat, t, d), F32),
        grid=(nmat, t // tm),
        in_specs=[pl.BlockSpec((tm, d), lambda j, i: (i, 0)),
                  _prev_spec(tm, d, 1),
                  pl.BlockSpec((None, 1, d), lambda j, i: (j, 0, 0)),
                  pl.BlockSpec((None, None, d, d), lambda j, i: (layer, j, 0, 0))],
        out_specs=pl.BlockSpec((None, tm, d), lambda j, i: (j, i, 0)),
        compiler_params=_params(("parallel", "parallel")),
        name="rw_proj",
    )(hn, hn, mu.reshape(nmat, 1, d), w_rkv)


def _rw_lora_kernel(x_ref, prev_ref, mu_ref, w0_ref, w1_ref, w2_ref, a0_ref, a1_ref, a2_ref,
                    g1_ref, g2_ref, lw_ref, a_ref, g_ref, *, blocks_per_seq):
    i = pl.program_id(0)
    x = x_ref[...].astype(F32)
    xx = _token_shift_delta(x, prev_ref, (i % blocks_per_seq) == 0)
    xw = (x + xx * mu_ref[0:1, :]).astype(BF16)
    xa = (x + xx * mu_ref[1:2, :]).astype(BF16)
    xg = (x + xx * mu_ref[2:3, :]).astype(BF16)
    hid = jnp.tanh(jnp.dot(xw, w1_ref[...], preferred_element_type=F32))
    w = w0_ref[...] + _dot(hid, w2_ref[...])
    lw_ref[...] = -0.6065306597126334 * jax.nn.sigmoid(w)
    hid = jnp.dot(xa, a1_ref[...], preferred_element_type=F32)
    a_ref[...] = jax.nn.sigmoid(a0_ref[...] + _dot(hid, a2_ref[...]))
    hid = jax.nn.sigmoid(jnp.dot(xg, g1_ref[...], preferred_element_type=F32))
    g_ref[...] = _dot(hid, g2_ref[...])


def rw_lora(hn, mu3, w0, w1, w2, a0, a1, a2, g1, g2, seq, tm=512):
    t, d = hn.shape
    row_spec = pl.BlockSpec((tm, d), lambda i: (i, 0))

    def full(arr):
        return pl.BlockSpec(arr.shape, lambda i: (0,) * arr.ndim)

    w0, a0 = w0.reshape(1, d), a0.reshape(1, d)
    consts = (mu3, w0, w1, w2, a0, a1, a2, g1, g2)
    return pl.pallas_call(
        functools.partial(_rw_lora_kernel, blocks_per_seq=seq // tm),
        out_shape=[jax.ShapeDtypeStruct((t, d), F32)] * 3,
        grid=(t // tm,),
        in_specs=[row_spec, _prev_spec(tm, d, 0)] + [full(c) for c in consts],
        out_specs=[row_spec] * 3,
        compiler_params=_params(("parallel",)),
        name="rw_lora",
    )(hn, hn, *consts)


def _split_terms(x, passes):
    terms = []
    for i in range(passes):
        part = x.astype(BF16)
        terms.append(part)
        if i + 1 < passes:
            x = x - part.astype(F32)
    return terms


def _unit_lower_inverse_many(lps, row, col, eye, mul):
    diag8 = jnp.right_shift(row, 3) == jnp.right_shift(col, 3)
    lds = [jnp.where(diag8, lp, 0.0) for lp in lps]
    l2s = [mul(ld, ld) for ld in lds]
    l4s = [mul(l2, l2) for l2 in l2s]
    ts = [mul(eye - ld, eye + l2) for ld, l2 in zip(lds, l2s)]
    ts = [mul(t, eye + l4) for t, l4 in zip(ts, l4s)]
    for sh in (3, 4, 5):
        rb, cb = jnp.right_shift(row, sh), jnp.right_shift(col, sh)
        sel = (jnp.right_shift(rb, 1) == jnp.right_shift(cb, 1)) & (rb != cb)
        ots = [mul(jnp.where(sel, lp, 0.0), t) for lp, t in zip(lps, ts)]
        ts = [t - mul(t, ot) for t, ot in zip(ts, ots)]
    return ts


def _rw_rec_kernel(r_ref, k_ref, v_ref, lw_ref, a_ref, g_ref, kk_ref, ka_ref, rk_ref,
                   lng_ref, lnb_ref, o_ref, s_ref, y_ref, *, n_chunks, n_pairs):
    c = pl.program_id(2)

    @pl.when(c == 0)
    def _():
        s_ref[...] = jnp.zeros_like(s_ref)

    tt, lwid = n_chunks * CHUNK, n_pairs * LANES
    n2 = 2 * CHUNK
    shift = CHUNK.bit_length() - 1
    row = lax.broadcasted_iota(jnp.int32, (CHUNK, n2), 0)
    col = jnp.bitwise_and(lax.broadcasted_iota(jnp.int32, (CHUNK, n2), 1), CHUNK - 1)
    strict = row > col
    incl = row >= col
    eye = (row == col).astype(F32)
    trow = lax.broadcasted_iota(jnp.int32, (tt, tt), 0)
    tcol = lax.broadcasted_iota(jnp.int32, (tt, tt), 1)
    tri = ((jnp.right_shift(trow, shift) == jnp.right_shift(tcol, shift))
           & (trow >= tcol)).astype(BF16)
    hw = min(lwid, MXU_DIM)
    hrow = lax.broadcasted_iota(jnp.int32, (hw, hw), 0)
    hcol = lax.broadcasted_iota(jnp.int32, (hw, hw), 1)
    head_ones = (jnp.right_shift(hrow, shift) == jnp.right_shift(hcol, shift)).astype(BF16)
    same_head = (jnp.right_shift(lax.broadcasted_iota(jnp.int32, (n2, n2), 0), shift)
                 == jnp.right_shift(lax.broadcasted_iota(jnp.int32, (n2, n2), 1), shift)).astype(F32)
    lane = lax.broadcasted_iota(jnp.int32, (CHUNK, n2), 1)
    m0b = (lane < RW_HEAD).astype(BF16)
    m1b = (lane >= RW_HEAD).astype(BF16)

    def head_sum(x, passes):
        cols = []
        for c0 in range(0, lwid, hw):
            cols.append(sum(jnp.dot(p, head_ones, preferred_element_type=F32)
                            for p in _split_terms(x[:, c0:c0 + hw], passes)))
        return jnp.concatenate(cols, axis=1)

    def stack_b(x):
        xb = x.astype(BF16)
        return jnp.concatenate([xb * m0b, xb * m1b], axis=0)

    r, k, v = r_ref[...], k_ref[...], v_ref[...]
    lw, a = lw_ref[...], a_ref[...]
    kkp = k * kk_ref[...]
    kap = kkp * lax.rsqrt(jnp.maximum(head_sum(kkp * kkp, 1), 1e-24))
    b = kap * a
    km = k * (1.0 + (a - 1.0) * ka_ref[...])
    cum = sum(jnp.dot(tri, p, preferred_element_type=F32) for p in _split_terms(lw, 2))
    e_in = jnp.exp(cum)
    e_neg = jnp.exp(-cum)
    kt = kap * jnp.exp(cum - lw)
    bt = b * e_neg
    kmt = km * e_neg
    rt = r * e_in

    items = [(ci, p) for ci in range(n_chunks) for p in range(n_pairs)]

    def blk(x, ci, p):
        return x[CHUNK * ci:CHUNK * (ci + 1), LANES * p:LANES * (p + 1)]

    def mul(a_sb, b_sb):
        return jnp.dot(a_sb.astype(BF16), stack_b(b_sb), preferred_element_type=F32)

    ktn = [blk(kt, *it) for it in items]
    rtn = [blk(rt, *it) for it in items]
    kts = [stack_b(x) for x in ktn]
    bts = [stack_b(blk(bt, *it)) for it in items]
    kms = [stack_b(blk(kmt, *it)) for it in items]
    vn = [blk(v, *it) for it in items]
    vsb = [stack_b(x) for x in vn]
    kbn = [jnp.concatenate([blk(kmt, *it), blk(bt, *it)], axis=0).astype(BF16) for it in items]
    pcs = [blk(e_in, *it)[CHUNK - 1:CHUNK, :] for it in items]
    kbs = [jnp.concatenate([km_, bt_], axis=0) for km_, bt_ in zip(kms, bts)]
    amats = [_dot_nt(jnp.concatenate([kt_, rt_], axis=0), kb)
             for kt_, rt_, kb in zip(ktn, rtn, kbs)]
    lps = [jnp.where(strict, am[:CHUNK, n2:], 0.0) for am in amats]
    mps = [jnp.where(strict, am[:CHUNK, :n2], 0.0).astype(BF16) for am in amats]
    arbs = [jnp.where(incl, am[CHUNK:, n2:], 0.0).astype(BF16) for am in amats]
    arks = [jnp.where(incl, am[CHUNK:, :n2], 0.0).astype(BF16) for am in amats]
    tps = [t.astype(BF16) for t in _unit_lower_inverse_many(lps, row, col, eye, mul)]
    mvs = [jnp.dot(mp, v_, preferred_element_type=F32) for mp, v_ in zip(mps, vsb)]
    wus = [jnp.dot(tp, jnp.concatenate([kt_, stack_b(mv)], axis=1),
                   preferred_element_type=F32)
           for tp, kt_, mv in zip(tps, kts, mvs)]
    awus = [jnp.dot(arb, jnp.concatenate([stack_b(wu[:, :n2]), stack_b(wu[:, n2:])], axis=1),
                    preferred_element_type=F32)
            for arb, wu in zip(arbs, wus)]
    qs = [(rt_ - awu[:, :n2]).astype(BF16) for rt_, awu in zip(rtn, awus)]
    ylocs = [jnp.dot(ark, v_, preferred_element_type=F32) - awu[:, n2:]
             for ark, v_, awu in zip(arks, vsb, awus)]
    gms = [(-_dot(wu[:, :n2].T, kb[CHUNK:]) * (same_head * pc)).astype(BF16)
           for wu, kb, pc in zip(wus, kbn, pcs)]
    hms = [_dot(jnp.concatenate([v_, -wu[:, n2:]], axis=0).T, kb) * (same_head * pc)
           for v_, wu, kb, pc in zip(vn, wus, kbn, pcs)]

    states = [s_ref[p] for p in range(n_pairs)]
    for idx, (ci, p) in enumerate(items):
        sb = states[p].astype(BF16)
        y_ref[CHUNK * ci:CHUNK * (ci + 1), LANES * p:LANES * (p + 1)] = (
            _dot_nt(qs[idx], sb) + ylocs[idx])
        states[p] = (states[p] * pcs[idx] + jnp.dot(sb, gms[idx], preferred_element_type=F32)
                     + hms[idx])
    for p in range(n_pairs):
        s_ref[p] = states[p]

    y = y_ref[...]
    dlt = y - head_sum(y, 1) * (1.0 / RW_HEAD)
    var = head_sum(dlt * dlt, 1) * (1.0 / RW_HEAD)
    yn = dlt * lax.rsqrt(var + GN_EPS) * lng_ref[...] + lnb_ref[...]
    bonus = head_sum(r * km * rk_ref[...], 1) * v
    o_ref[...] = ((yn + bonus) * g_ref[...]).astype(o_ref.dtype)


def rw_rec(rkv, lw, a, g, k_k, k_a, r_k, ln_g, ln_b, batch, n_chunks=8, n_pairs=4):
    _, t, d = rkv.shape
    seq = t // batch
    tt = n_chunks * CHUNK
    lwid = n_pairs * LANES
    nc = seq // tt

    def rkv_spec(m):
        return pl.BlockSpec((None, tt, lwid), lambda b, l, c: (m, b * nc + c, l))

    act_spec = pl.BlockSpec((tt, lwid), lambda b, l, c: (b * nc + c, l))
    vec_spec = pl.BlockSpec((1, lwid), lambda b, l, c: (0, l))
    vecs = [x.reshape(1, d) for x in (k_k, k_a, r_k, ln_g, ln_b)]
    return pl.pallas_call(
        functools.partial(_rw_rec_kernel, n_chunks=n_chunks, n_pairs=n_pairs),
        out_shape=jax.ShapeDtypeStruct((t, d), BF16),
        grid=(batch, d // lwid, nc),
        in_specs=[rkv_spec(0), rkv_spec(1), rkv_spec(2), act_spec, act_spec, act_spec]
                 + [vec_spec] * 5,
        out_specs=act_spec,
        scratch_shapes=[pltpu.VMEM((n_pairs, LANES, LANES), F32), pltpu.VMEM((tt, lwid), F32)],
        compiler_params=_params(("parallel", "parallel", "arbitrary")),
        name="rw_rec",
    )(rkv, rkv, rkv, lw, a, g, *vecs)


def _pad_axis(x, axis, size):
    pad = [(0, 0)] * x.ndim
    pad[axis] = (0, size - x.shape[axis])
    return jnp.pad(x, pad)


def kernel(x, norm_mix_g, norm_mlp_g, w_mlp_up, w_mlp_down, final_norm_g, rg_w_in, rg_conv_w, rg_conv_b, rg_gx_w, rg_gx_b, rg_ga_w, rg_ga_b, rg_lambda, rg_w_out, rw_mu, rw_w_rkv, rw_w0, rw_w1, rw_w2, rw_a0, rw_a1, rw_a2, rw_g1, rw_g2, rw_k_k, rw_k_a, rw_r_k, rw_ln_g, rw_ln_b, rw_w_out):
    batch, seq, d = x.shape
    depth = norm_mix_g.shape[0]
    t = batch * seq
    h = x.reshape(t, d)
    rg_w_out_b, rw_w_out_b = rg_w_out.astype(BF16), rw_w_out.astype(BF16)
    rg_gx_b16, rg_ga_b16 = rg_gx_w.astype(BF16), rg_ga_w.astype(BF16)
    rw_w_rkv_b = rw_w_rkv.astype(BF16)
    hn = rmsnorm(h, norm_mix_g[0], BF16)
    for i in range(depth):
        j = i // 2
        if i % 2 == 0:
            gate = mm_act(hn, rg_w_in, j, "gelu", F32, col0=0, n=d)
            rec = mm_act(hn, rg_w_in, j, "none", F32, col0=d, n=d)
            y = rg_scan(rec, gate, rg_conv_w[j], rg_conv_b[j], rg_gx_b16, rg_gx_b[j],
                        rg_ga_b16, rg_ga_b[j], rg_lambda[j], j, batch)
            w_out = rg_w_out_b
        else:
            rkv = rw_proj(hn, rw_mu[j, :3], rw_w_rkv_b, j, seq)
            lora = LANES
            lw, a, g = rw_lora(
                hn, rw_mu[j, 3:], rw_w0[j],
                _pad_axis(rw_w1[j], 1, lora).astype(BF16), _pad_axis(rw_w2[j], 0, lora).astype(BF16),
                rw_a0[j],
                _pad_axis(rw_a1[j], 1, lora).astype(BF16), _pad_axis(rw_a2[j], 0, lora).astype(BF16),
                rw_g1[j].astype(BF16), rw_g2[j].astype(BF16), seq)
            y = rw_rec(rkv, lw, a, g, rw_k_k[j], rw_k_a[j], rw_r_k[j].reshape(d),
                       rw_ln_g[j], rw_ln_b[j], batch)
            w_out = rw_w_out_b
        h, hn = mm_res_norm(y, w_out, j, h, norm_mlp_g[i], BF16, tk=d)
        u, w_down_b = mm_act(hn, w_mlp_up, i, "relu2", BF16, cast=w_mlp_down)
        last = i + 1 == depth
        g_next = final_norm_g if last else norm_mix_g[i + 1]
        h, hn = mm_res_norm(u, w_down_b, None, h, g_next, F32 if last else BF16, emit_h=not last,
                            tm=1024, tk=1024)
    return hn.reshape(batch, seq, d)
```
